```python
import jax, jax.numpy as jnp
from jax import lax
import numpy as np

D_MODEL = 1024
BATCH = 8
SEQ = 8192
DEPTH = 4

D_A = D_MODEL
G_A = 8
GROUP_A = D_A // G_A
CHUNK_A = 128
H_B = 8
DK_B = D_MODEL // H_B
DV_B = D_MODEL // H_B
D_B = H_B * DK_B
CHUNK_B = 64
D_FF = 4 * D_MODEL
IN_WIDTHS = (D_A, D_A, D_B, D_B, D_B, D_B, D_MODEL, D_MODEL)
N_IN = 2 * D_A + 4 * D_B + 2 * D_MODEL
EPS = 1e-6

kernel_name = 'hybrid_sgu_hgrn2_block'


def rms_norm(x, g):
    xf = x.astype(jnp.float32)
    y = xf * lax.rsqrt(jnp.mean(xf * xf, axis=-1, keepdims=True) + EPS)
    return (y * g.astype(jnp.float32)).astype(x.dtype)


def layer_norm(x, g, b):
    xf = x.astype(jnp.float32)
    mu = jnp.mean(xf, axis=-1, keepdims=True)
    xc = xf - mu
    y = xc * lax.rsqrt(jnp.mean(xc * xc, axis=-1, keepdims=True) + EPS)
    return (y * g.astype(jnp.float32) + b.astype(jnp.float32)).astype(x.dtype)


def spatial_gating(u, v, ln_g, ln_b, w_s, b_s):
    B, S, _ = v.shape
    v = layer_norm(v, ln_g, ln_b)
    v = v.reshape(B, S // CHUNK_A, CHUNK_A, G_A, GROUP_A)
    causal = jnp.tril(jnp.ones((CHUNK_A, CHUNK_A), dtype=bool))
    w = jnp.where(causal[None], w_s, jnp.zeros_like(w_s))
    mixed = jnp.einsum('gts,bnsgc->bntgc', w, v) + jnp.transpose(b_s)[None, None, :, :, None]
    return u * mixed.reshape(B, S, D_A)


def hgrn2_chunk_step(state, inputs):
    qc, lfc, kc, vc = inputs
    bcum = jnp.cumsum(lfc, axis=-2)
    causal = jnp.tril(jnp.ones((CHUNK_B, CHUNK_B), dtype=bool))[:, :, None]
    diff = bcum[..., :, None, :] - bcum[..., None, :, :]
    decay = jnp.where(causal, jnp.exp(jnp.where(causal, diff, 0.0)), 0.0)
    scores = jnp.einsum('bhtk,bhsk,bhtsk->bhts', qc, kc, decay)
    o_intra = jnp.einsum('bhts,bhsv->bhtv', scores, vc)
    o_inter = jnp.einsum('bhtk,bhkv->bhtv', qc * jnp.exp(bcum), state)
    b_last = bcum[..., -1:, :]
    new_state = (jnp.exp(b_last[..., 0, :])[..., None] * state
                 + jnp.einsum('bhsk,bhsv->bhkv', kc * jnp.exp(b_last - bcum), vc))
    return new_state, o_intra + o_inter


def hgrn2_mix(q_raw, f_raw, i_raw, g_raw, lb, norm_g):
    B, S, _ = q_raw.shape
    nc = S // CHUNK_B
    q = jax.nn.silu(q_raw.astype(jnp.float32))
    lbf = lb.astype(jnp.float32)
    log_f = jnp.logaddexp(jnp.log(lbf), jnp.log1p(-lbf) + jax.nn.log_sigmoid(f_raw.astype(jnp.float32)))
    k = -jnp.expm1(log_f)
    v = i_raw.astype(jnp.float32)

    def to_chunks(t, dh):
        return t.reshape(B, nc, CHUNK_B, H_B, dh).transpose(1, 0, 3, 2, 4)

    xs = (to_chunks(q, DK_B), to_chunks(log_f, DK_B), to_chunks(k, DK_B), to_chunks(v, DV_B))
    s0 = jnp.zeros((B, H_B, DK_B, DV_B), jnp.float32)
    _, o = lax.scan(hgrn2_chunk_step, s0, xs)
    o = o.transpose(1, 0, 3, 2, 4).reshape(B, S, H_B, DV_B)
    gate = jax.nn.silu(g_raw.astype(jnp.float32)).reshape(B, S, H_B, DV_B)
    o = rms_norm(o, norm_g) * gate
    return o.reshape(B, S, D_B).astype(q_raw.dtype)


def setup_inputs(seed: int = 0) -> dict:
    key = jax.random.key(seed)
    ks = jax.random.split(key, 20)
    f32 = jnp.float32

    def nrm(k, shape, scale):
        return jax.random.normal(k, shape, f32) * scale

    return {
        'x': jax.random.normal(ks[0], (BATCH, SEQ, D_MODEL), f32),
        'mix_norm_g': 1.0 + nrm(ks[1], (DEPTH, D_MODEL), 0.02),
        'w_in': nrm(ks[2], (DEPTH, D_MODEL, N_IN), D_MODEL ** -0.5),
        'sgu_norm_g': 1.0 + nrm(ks[3], (DEPTH, D_A), 0.02),
        'sgu_norm_b': nrm(ks[4], (DEPTH, D_A), 0.02),
        'w_spatial': nrm(ks[5], (DEPTH, G_A, CHUNK_A, CHUNK_A), CHUNK_A ** -0.5),
        'b_spatial': 1.0 + nrm(ks[6], (DEPTH, G_A, CHUNK_A), 0.02),
        'lower_bounds': nrm(ks[7], (DEPTH, D_B), 0.1),
        'hgrn_norm_g': 1.0 + nrm(ks[8], (DEPTH, DV_B), 0.02),
        'w_branch_a': nrm(ks[9], (DEPTH, D_A, D_MODEL), D_A ** -0.5),
        'w_branch_b': nrm(ks[10], (DEPTH, D_B, D_MODEL), D_B ** -0.5),
        'w_out': nrm(ks[11], (DEPTH, D_MODEL, D_MODEL), D_MODEL ** -0.5),
        'mlp_norm_g': 1.0 + nrm(ks[12], (DEPTH, D_MODEL), 0.02),
        'w_mlp_up': nrm(ks[13], (DEPTH, D_MODEL, D_FF), D_MODEL ** -0.5),
        'w_mlp_down': nrm(ks[14], (DEPTH, D_FF, D_MODEL), D_FF ** -0.5),
        'final_norm_g': 1.0 + nrm(ks[15], (D_MODEL,), 0.02),
    }


def reference(x, mix_norm_g, w_in, sgu_norm_g, sgu_norm_b, w_spatial, b_spatial, lower_bounds,
              hgrn_norm_g, w_branch_a, w_branch_b, w_out, mlp_norm_g, w_mlp_up, w_mlp_down,
              final_norm_g):
    lb_all = jnp.cumsum(jax.nn.softmax(lower_bounds.astype(jnp.float32), axis=0), axis=0)
    lb_all = lb_all - lb_all[0:1]
    for l in range(DEPTH):
        h = rms_norm(x, mix_norm_g[l])
        z = h @ w_in[l]
        parts = []
        start = 0
        for w in IN_WIDTHS:
            parts.append(z[..., start:start + w])
            start += w
        u, v, q_raw, f_raw, i_raw, g_raw, gate_a, gate_b = parts
        a = spatial_gating(jax.nn.gelu(u), jax.nn.gelu(v), sgu_norm_g[l], sgu_norm_b[l],
                           w_spatial[l], b_spatial[l])
        b = hgrn2_mix(q_raw, f_raw, i_raw, g_raw, lb_all[l], hgrn_norm_g[l])
        merged = (jax.nn.sigmoid(gate_a) * (a @ w_branch_a[l])
                  + jax.nn.sigmoid(gate_b) * (b @ w_branch_b[l]))
        x = x + merged @ w_out[l]
        h = rms_norm(x, mlp_norm_g[l])
        x = x + jnp.square(jax.nn.relu(h @ w_mlp_up[l])) @ w_mlp_down[l]
    return rms_norm(x, final_norm_g)
```

```python
import functools
import math

import jax
import jax.numpy as jnp
import numpy as np
from jax import lax
from jax.experimental import pallas as pl
from jax.experimental.pallas import tpu as pltpu

EPS = 1e-6
GROUPS = 8
HEAD = 128
CHUNK = 128
N_BRANCH = 8
ROW_BLOCK = 256
IN_PROJ_ROWS = 512
MLP_ROWS = 512
VMEM_LIMIT = 56 * 1024 * 1024

_NT = (((1,), (1,)), ((), ()))
_TN = (((0,), (0,)), ((), ()))


def _f32(x):
    return x.astype(jnp.float32)


def _bf16(x):
    return x.astype(jnp.bfloat16)


def _sigmoid(x):
    return 0.5 * (1.0 + jnp.tanh(0.5 * x))


def _gelu(x):
    c = math.sqrt(2.0 / math.pi)
    return 0.5 * x * (1.0 + jnp.tanh(c * (x + 0.044715 * (x * x * x))))


def _silu(x):
    return x * _sigmoid(x)


def _dot(a, b):
    return jnp.dot(a, b, preferred_element_type=jnp.float32)


def _resident(shape):
    nd = len(shape)
    return pl.BlockSpec(shape, lambda *_: (0,) * nd, pipeline_mode=pl.Buffered(1))


def _in_proj_kernel(x_ref, g_ref, w_ref, lng_ref, lnb_ref, loglb_ref, l1mlb_ref, omlb_ref,
                    zb_ref, lf_ref, *, d):
    rows = x_ref.shape[0]
    for r0 in range(0, rows, ROW_BLOCK):
        rs = slice(r0, r0 + ROW_BLOCK)
        x = x_ref[rs, :]
        h = x * lax.rsqrt(jnp.mean(x * x, axis=-1, keepdims=True) + EPS) * g_ref[...]
        hb = _bf16(h)
        for c in range(N_BRANCH):
            cs = slice(c * d, (c + 1) * d)
            z = _dot(hb, w_ref[:, cs])
            if c == 0:
                out = _gelu(z)
            elif c == 1:
                gv = _gelu(z)
                mu = jnp.mean(gv, axis=-1, keepdims=True)
                xc = gv - mu
                var = jnp.mean(xc * xc, axis=-1, keepdims=True)
                out = xc * lax.rsqrt(var + EPS) * lng_ref[...] + lnb_ref[...]
            elif c in (2, 5):
                out = _silu(z)
            elif c == 3:
                ez = jnp.exp(-jnp.abs(z))
                log_sig = jnp.minimum(z, 0.0) - jnp.log1p(ez)
                a = l1mlb_ref[...] + log_sig
                b = loglb_ref[...]
                lf = jnp.maximum(a, b) + jnp.log1p(jnp.exp(-jnp.abs(a - b)))
                lf_ref[rs, :] = lf
                out = omlb_ref[...] * (jnp.where(z >= 0.0, ez, 1.0) / (1.0 + ez))
            elif c == 4:
                out = z
            else:
                out = _sigmoid(z)
            zb_ref[rs, cs] = _bf16(out)


def _in_proj(x2, g, w, lng, lnb, loglb, l1mlb, omlb):
    m, d = x2.shape
    n = w.shape[1]
    rows = min(IN_PROJ_ROWS, m)
    row = lambda i: (i, 0)
    return pl.pallas_call(
        functools.partial(_in_proj_kernel, d=d),
        grid=(m // rows,),
        in_specs=[
            pl.BlockSpec((rows, d), row),
            _resident((1, d)), _resident((d, n)), _resident((1, d)), _resident((1, d)),
            _resident((1, d)), _resident((1, d)), _resident((1, d)),
        ],
        out_specs=[pl.BlockSpec((rows, n), row), pl.BlockSpec((rows, d), row)],
        out_shape=[jax.ShapeDtypeStruct((m, n), jnp.bfloat16),
                   jax.ShapeDtypeStruct((m, d), jnp.float32)],
        compiler_params=pltpu.CompilerParams(
            dimension_semantics=("arbitrary",), vmem_limit_bytes=VMEM_LIMIT),
        name="in_proj",
    )(x2, g, w, lng, lnb, loglb, l1mlb, omlb)


def _level_halves(t):
    h = t // 2
    out = []
    while h >= 1:
        out.append(h)
        h //= 2
    return out


def _decay_matrix(t):
    tri = np.tril(np.ones((t, t), np.float32))
    idx = np.arange(t)
    mats = [tri]
    for h in _level_halves(t):
        ref = (idx // (2 * h)) * (2 * h) + h - 1
        mats.append(tri - tri[ref])
    mats.append(tri[t - 1][None, :] - tri)
    m = np.concatenate(mats, axis=0)
    return np.concatenate([m, m], axis=1)


def _level_masks(t):
    idx = np.arange(t)
    masks = [np.eye(t, dtype=np.float32)]
    for h in _level_halves(t):
        blk = idx // (2 * h)
        second = (idx % (2 * h)) >= h
        masks.append((blk[:, None] == blk[None, :]) & second[:, None] & ~second[None, :])
    return np.stack(masks).astype(np.float32)


def _mixer_kernel(x_ref, zb_ref, lf_ref, dm_ref, mask_ref, ws_ref, bst_ref, ng_ref,
                  wa_ref, wb_ref, wo_ref, out_ref, state_ref, a_ref, b_ref, *, d):
    t = x_ref.shape[0]
    n_lev = mask_ref.shape[0] - 1

    @pl.when(pl.program_id(1) == 0)
    def _():
        state_ref[...] = jnp.zeros_like(state_ref)

    def zcol(c, g):
        lo = c * d + g * HEAD
        return zb_ref[:, lo:lo + HEAD]

    causal = mask_ref[0] > 0.0
    for lev in range(1, n_lev + 1):
        causal = causal | (mask_ref[lev] > 0.0)
    for g in range(GROUPS):
        w = _bf16(jnp.where(causal, ws_ref[g], 0.0))
        mixed = _dot(w, zcol(1, g)) + bst_ref[:, g:g + 1]
        a_ref[:, g * HEAD:(g + 1) * HEAD] = _bf16(_f32(zcol(0, g)) * mixed)

    lf = lf_ref[...]
    hi = _bf16(lf)
    lo = _bf16(lf - _f32(hi))
    dec = _dot(dm_ref[...], jnp.concatenate([hi, lo], axis=0))
    for g in range(GROUPS):
        hs = slice(g * HEAD, (g + 1) * HEAD)
        q = _f32(zcol(2, g))
        kb = zcol(3, g)
        k = _f32(kb)
        vb = zcol(4, g)
        scores = _dot_nt(_bf16(q), kb) * mask_ref[0]
        for lev in range(n_lev):
            e = jnp.exp(-jnp.abs(dec[(lev + 1) * t:(lev + 2) * t, hs]))
            s = _dot_nt(_bf16(q * e), _bf16(k * e))
            scores = scores + s * mask_ref[lev + 1]
        bcum = dec[0:t, hs]
        st = state_ref[g]
        o = _dot(_bf16(scores), vb) + _dot_nt(_bf16(q * jnp.exp(bcum)), _bf16(st))
        k_end = _bf16(k * jnp.exp(dec[(n_lev + 1) * t:(n_lev + 2) * t, hs]))
        state_ref[g] = st * jnp.exp(bcum[t - 1:t, :]) + lax.dot_general(
            vb, k_end, _TN, preferred_element_type=jnp.float32)
        o = o * lax.rsqrt(jnp.mean(o * o, axis=-1, keepdims=True) + EPS) * ng_ref[...]
        b_ref[:, hs] = _bf16(o * _f32(zcol(5, g)))

    pa = _dot(a_ref[...], wa_ref[...])
    pb = _dot(b_ref[...], wb_ref[...])
    merged = _f32(zb_ref[:, 6 * d:7 * d]) * pa + _f32(zb_ref[:, 7 * d:8 * d]) * pb
    out_ref[...] = x_ref[...] + _dot(_bf16(merged), wo_ref[...])


def _dot_nt(a, b):
    return lax.dot_general(a, b, _NT, preferred_element_type=jnp.float32)


def _mixer(x2, zb, lf, dm, masks, ws, bst, ng, wa, wb, wo, *, batch):
    m, d = x2.shape
    n = zb.shape[1]
    t = CHUNK
    steps = m // batch // t
    row = lambda b, s: (b * steps + s, 0)
    return pl.pallas_call(
        functools.partial(_mixer_kernel, d=d),
        grid=(batch, steps),
        in_specs=[
            pl.BlockSpec((t, d), row), pl.BlockSpec((t, n), row), pl.BlockSpec((t, d), row),
            _resident(dm.shape), _resident(masks.shape), _resident(ws.shape),
            _resident(bst.shape), _resident(ng.shape),
            _resident(wa.shape), _resident(wb.shape), _resident(wo.shape),
        ],
        out_specs=pl.BlockSpec((t, d), row),
        out_shape=jax.ShapeDtypeStruct((m, d), jnp.float32),
        scratch_shapes=[
            pltpu.VMEM((GROUPS, HEAD, HEAD), jnp.float32),
            pltpu.VMEM((t, d), jnp.bfloat16),
            pltpu.VMEM((t, d), jnp.bfloat16),
        ],
        compiler_params=pltpu.CompilerParams(
            dimension_semantics=("arbitrary", "arbitrary"), vmem_limit_bytes=VMEM_LIMIT),
        name="mixer",
    )(x2, zb, lf, dm, masks, ws, bst, ng, wa, wb, wo)


def _mlp_kernel(x_ref, g_ref, wu_ref, wd_ref, fg_ref, out_ref, *, final_norm):
    rows = x_ref.shape[0]
    for r0 in range(0, rows, ROW_BLOCK):
        rs = slice(r0, r0 + ROW_BLOCK)
        x = x_ref[rs, :]
        h = x * lax.rsqrt(jnp.mean(x * x, axis=-1, keepdims=True) + EPS) * g_ref[...]
        up = jnp.maximum(_dot(_bf16(h), wu_ref[...]), 0.0)
        y = x + _dot(_bf16(up * up), wd_ref[...])
        if final_norm:
            y = y * lax.rsqrt(jnp.mean(y * y, axis=-1, keepdims=True) + EPS) * fg_ref[...]
        out_ref[rs, :] = y


def _mlp(x2, g, wu, wd, fg, *, final_norm):
    m, d = x2.shape
    rows = min(MLP_ROWS, m)
    row = lambda i: (i, 0)
    return pl.pallas_call(
        functools.partial(_mlp_kernel, final_norm=final_norm),
        grid=(m // rows,),
        in_specs=[pl.BlockSpec((rows, d), row), _resident((1, d)), _resident(wu.shape),
                  _resident(wd.shape), _resident((1, d))],
        out_specs=pl.BlockSpec((rows, d), row),
        out_shape=jax.ShapeDtypeStruct((m, d), jnp.float32),
        compiler_params=pltpu.CompilerParams(
            dimension_semantics=("arbitrary",), vmem_limit_bytes=VMEM_LIMIT),
        name="mlp",
    )(x2, g, wu, wd, fg)


def kernel(x, mix_norm_g, w_in, sgu_norm_g, sgu_norm_b, w_spatial, b_spatial, lower_bounds,
           hgrn_norm_g, w_branch_a, w_branch_b, w_out, mlp_norm_g, w_mlp_up, w_mlp_down,
           final_norm_g):
    batch, seq, d = x.shape
    depth = w_in.shape[0]
    assert d == GROUPS * HEAD and seq % CHUNK == 0 and w_in.shape[2] == N_BRANCH * d

    lb = jnp.cumsum(jax.nn.softmax(_f32(lower_bounds), axis=0), axis=0)
    lb = lb - lb[0:1]
    loglb = jnp.log(lb)
    l1mlb = jnp.log1p(-lb)

    dm = jnp.asarray(_decay_matrix(CHUNK), jnp.bfloat16)
    masks = jnp.asarray(_level_masks(CHUNK), jnp.float32)

    row = lambda p, l: p[l].reshape(1, -1)
    x2 = x.reshape(batch * seq, d)
    for l in range(depth):
        zb, lf = _in_proj(x2, row(mix_norm_g, l), _bf16(w_in[l]), row(sgu_norm_g, l),
                          row(sgu_norm_b, l), row(loglb, l), row(l1mlb, l), row(1.0 - lb, l))
        x2 = _mixer(x2, zb, lf, dm, masks, w_spatial[l], b_spatial[l].T,
                    row(hgrn_norm_g, l), _bf16(w_branch_a[l]), _bf16(w_branch_b[l]),
                    _bf16(w_out[l]), batch=batch)
        x2 = _mlp(x2, row(mlp_norm_g, l), _bf16(w_mlp_up[l]), _bf16(w_mlp_down[l]),
                  final_norm_g.reshape(1, -1), final_norm=(l == depth - 1))
    return x2.reshape(batch, seq, d)
```

```python
import functools
import math

import jax
import jax.numpy as jnp
import numpy as np
from jax import lax
from jax.experimental import pallas as pl
from jax.experimental.pallas import tpu as pltpu

EPS = 1e-6
GROUPS = 8
HEAD = 128
CHUNK = 128
SUBLANES = 8
N_BRANCH = 8
ROW_BLOCK = 256
IN_PROJ_ROWS = 512
MIXER_ROWS = 256
MLP_ROWS = 512
VMEM_LIMIT = 56 * 1024 * 1024

_NT = (((1,), (1,)), ((), ()))
_TN = (((0,), (0,)), ((), ()))


def _f32(x):
    return x.astype(jnp.float32)


def _bf16(x):
    return x.astype(jnp.bfloat16)


def _sigmoid(x):
    return 0.5 * (1.0 + jnp.tanh(0.5 * x))


def _gelu(x):
    c = math.sqrt(2.0 / math.pi)
    return 0.5 * x * (1.0 + jnp.tanh(c * (x + 0.044715 * (x * x * x))))


def _silu(x):
    return x * _sigmoid(x)


def _dot(a, b):
    return jnp.dot(a, b, preferred_element_type=jnp.float32)


def _dot_nt(a, b):
    return lax.dot_general(a, b, _NT, preferred_element_type=jnp.float32)


def _dot_tn(a, b):
    return lax.dot_general(a, b, _TN, preferred_element_type=jnp.float32)


def _resident(shape):
    nd = len(shape)
    return pl.BlockSpec(shape, lambda *_: (0,) * nd, pipeline_mode=pl.Buffered(1))


def _in_proj_kernel(x_ref, g_ref, w_ref, lng_ref, lnb_ref, loglb_ref, l1mlb_ref, omlb_ref,
                    zb_ref, lf_ref, *, d):
    rows = x_ref.shape[0]
    for r0 in range(0, rows, ROW_BLOCK):
        rs = slice(r0, r0 + ROW_BLOCK)
        x = x_ref[rs, :]
        h = x * lax.rsqrt(jnp.mean(x * x, axis=-1, keepdims=True) + EPS) * g_ref[...]
        hb = _bf16(h)
        for c in range(N_BRANCH):
            cs = slice(c * d, (c + 1) * d)
            z = _dot(hb, w_ref[:, cs])
            if c == 0:
                out = _gelu(z)
            elif c == 1:
                gv = _gelu(z)
                mu = jnp.mean(gv, axis=-1, keepdims=True)
                xc = gv - mu
                var = jnp.mean(xc * xc, axis=-1, keepdims=True)
                out = xc * lax.rsqrt(var + EPS) * lng_ref[...] + lnb_ref[...]
            elif c in (2, 5):
                out = _silu(z)
            elif c == 3:
                ez = jnp.exp(-jnp.abs(z))
                log_sig = jnp.minimum(z, 0.0) - jnp.log1p(ez)
                a = l1mlb_ref[...] + log_sig
                b = loglb_ref[...]
                lf = jnp.maximum(a, b) + jnp.log1p(jnp.exp(-jnp.abs(a - b)))
                lf_ref[rs, :] = lf
                out = omlb_ref[...] * (jnp.where(z >= 0.0, ez, 1.0) / (1.0 + ez))
            elif c == 4:
                out = z
            else:
                out = _sigmoid(z)
            zb_ref[rs, cs] = _bf16(out)


def _in_proj(x2, g, w, lng, lnb, loglb, l1mlb, omlb):
    m, d = x2.shape
    n = w.shape[1]
    rows = min(IN_PROJ_ROWS, m)
    row = lambda i: (i, 0)
    return pl.pallas_call(
        functools.partial(_in_proj_kernel, d=d),
        grid=(m // rows,),
        in_specs=[
            pl.BlockSpec((rows, d), row),
            _resident((1, d)), _resident((d, n)), _resident((1, d)), _resident((1, d)),
            _resident((1, d)), _resident((1, d)), _resident((1, d)),
        ],
        out_specs=[pl.BlockSpec((rows, n), row), pl.BlockSpec((rows, d), row)],
        out_shape=[jax.ShapeDtypeStruct((m, n), jnp.bfloat16),
                   jax.ShapeDtypeStruct((m, d), jnp.float32)],
        compiler_params=pltpu.CompilerParams(
            dimension_semantics=("arbitrary",), vmem_limit_bytes=VMEM_LIMIT),
        name="in_proj",
    )(x2, g, w, lng, lnb, loglb, l1mlb, omlb)


def _level_halves(t):
    h = t // 2
    out = []
    while h >= 1:
        out.append(h)
        h //= 2
    return out


def _cumsum_matrix(t):
    tri = np.tril(np.ones((t, t), np.float32))
    return np.concatenate([tri, tri], axis=1)


def _level_masks(t):
    idx = np.arange(t)
    masks = []
    for h in _level_halves(t):
        blk = idx // (2 * h)
        second = (idx % (2 * h)) >= h
        masks.append((blk[:, None] == blk[None, :]) & second[:, None] & ~second[None, :])
    return np.stack(masks).astype(np.float32)


def _mixer_kernel(x_ref, zb_ref, lf_ref, tri_ref, mask_ref, ws_ref, bst_ref, ng_ref,
                  wa_ref, wb_ref, wo_ref, out_ref,
                  state_ref, wsm_ref, bc_ref, a_ref, b_ref, *, d):
    rows = x_ref.shape[0]
    t = CHUNK
    halves = _level_halves(t)

    @pl.when((pl.program_id(0) == 0) & (pl.program_id(1) == 0))
    def _():
        tok = lax.broadcasted_iota(jnp.int32, (t, t), 0)
        src = lax.broadcasted_iota(jnp.int32, (t, t), 1)
        for g in range(GROUPS):
            wsm_ref[g] = _bf16(jnp.where(tok >= src, ws_ref[g], 0.0))

    @pl.when(pl.program_id(1) == 0)
    def _():
        state_ref[...] = jnp.zeros_like(state_ref)

    for r0 in range(0, rows, t):
        lf = lf_ref[r0:r0 + t, :]
        hi = _bf16(lf)
        lo = _bf16(lf - _f32(hi))
        bc_ref[r0:r0 + t, :] = _dot(tri_ref[...], jnp.concatenate([hi, lo], axis=0))

    sub = lax.broadcasted_iota(jnp.int32, (t, HEAD), 0)

    def bcast_row(r, n, hs):
        return jnp.broadcast_to(bc_ref[r:r + 1, hs], (n, HEAD))

    for r0 in range(0, rows, t):
        rs = slice(r0, r0 + t)

        def zcol(c, g):
            lo_ = c * d + g * HEAD
            return zb_ref[rs, lo_:lo_ + HEAD]

        for g in range(GROUPS):
            mixed = _dot(wsm_ref[g], zcol(1, g)) + bst_ref[:, g:g + 1]
            a_ref[rs, g * HEAD:(g + 1) * HEAD] = _bf16(_f32(zcol(0, g)) * mixed)

        for g in range(GROUPS):
            hs = slice(g * HEAD, (g + 1) * HEAD)
            qb = zcol(2, g)
            kb = zcol(3, g)
            vb = zcol(4, g)
            bcum = bc_ref[rs, hs]
            scores = None
            for lev, h in enumerate(halves):
                if h == 1:
                    nabs = jnp.where((sub & 1) == 1, lf_ref[rs, hs], 0.0)
                else:
                    if 2 * h >= SUBLANES:
                        parts = [bcast_row(r0 + blk * 2 * h + h - 1, 2 * h, hs)
                                 for blk in range(t // (2 * h))]
                    else:
                        parts = []
                        for v0 in range(0, t, SUBLANES):
                            first = bcast_row(r0 + v0 + h - 1, SUBLANES, hs)
                            second = bcast_row(r0 + v0 + 2 * h + h - 1, SUBLANES, hs)
                            parts.append(jnp.where((sub[:SUBLANES] & (2 * h)) == 0, first, second))
                    nabs = -jnp.abs(bcum - jnp.concatenate(parts, axis=0))
                eb = _bf16(jnp.exp(nabs))
                s = _dot_nt(qb * eb, kb * eb) * mask_ref[lev]
                scores = s if scores is None else scores + s
            st = state_ref[g]
            q_in = qb * _bf16(jnp.exp(bcum))
            diag = jnp.sum(_f32(qb) * _f32(kb), axis=-1, keepdims=True)
            o = _dot(_bf16(scores), vb) + _dot_nt(q_in, _bf16(st)) + diag * _f32(vb)
            b_last = bcast_row(r0 + t - 1, t, hs)
            k_end = kb * _bf16(jnp.exp(b_last - bcum))
            state_ref[g] = st * jnp.exp(bcum[t - 1:t, :]) + _dot_tn(vb, k_end)
            o = o * lax.rsqrt(jnp.mean(o * o, axis=-1, keepdims=True) + EPS) * ng_ref[...]
            b_ref[rs, hs] = _bf16(o * _f32(zcol(5, g)))

    pa = _dot(a_ref[...], wa_ref[...])
    pb = _dot(b_ref[...], wb_ref[...])
    merged = _f32(zb_ref[:, 6 * d:7 * d]) * pa + _f32(zb_ref[:, 7 * d:8 * d]) * pb
    out_ref[...] = x_ref[...] + _dot(_bf16(merged), wo_ref[...])


def _mixer(x2, zb, lf, tri, masks, ws, bst, ng, wa, wb, wo, *, batch):
    m, d = x2.shape
    n = zb.shape[1]
    rows = MIXER_ROWS
    steps = m // batch // rows
    row = lambda b, s: (b * steps + s, 0)
    return pl.pallas_call(
        functools.partial(_mixer_kernel, d=d),
        grid=(batch, steps),
        in_specs=[
            pl.BlockSpec((rows, d), row), pl.BlockSpec((rows, n), row),
            pl.BlockSpec((rows, d), row),
            _resident(tri.shape), _resident(masks.shape), _resident(ws.shape),
            _resident(bst.shape), _resident(ng.shape),
            _resident(wa.shape), _resident(wb.shape), _resident(wo.shape),
        ],
        out_specs=pl.BlockSpec((rows, d), row),
        out_shape=jax.ShapeDtypeStruct((m, d), jnp.float32),
        scratch_shapes=[
            pltpu.VMEM((GROUPS, HEAD, HEAD), jnp.float32),
            pltpu.VMEM((GROUPS, CHUNK, CHUNK), jnp.bfloat16),
            pltpu.VMEM((rows, d), jnp.float32),
            pltpu.VMEM((rows, d), jnp.bfloat16),
            pltpu.VMEM((rows, d), jnp.bfloat16),
        ],
        compiler_params=pltpu.CompilerParams(
            dimension_semantics=("arbitrary", "arbitrary"), vmem_limit_bytes=VMEM_LIMIT),
        name="mixer",
    )(x2, zb, lf, tri, masks, ws, bst, ng, wa, wb, wo)


def _mlp_kernel(x_ref, g_ref, wu_ref, wd_ref, fg_ref, out_ref, *, final_norm):
    rows = x_ref.shape[0]
    for r0 in range(0, rows, ROW_BLOCK):
        rs = slice(r0, r0 + ROW_BLOCK)
        x = x_ref[rs, :]
        h = x * lax.rsqrt(jnp.mean(x * x, axis=-1, keepdims=True) + EPS) * g_ref[...]
        up = jnp.maximum(_dot(_bf16(h), wu_ref[...]), 0.0)
        y = x + _dot(_bf16(up * up), wd_ref[...])
        if final_norm:
            y = y * lax.rsqrt(jnp.mean(y * y, axis=-1, keepdims=True) + EPS) * fg_ref[...]
        out_ref[rs, :] = y


def _mlp(x2, g, wu, wd, fg, *, final_norm):
    m, d = x2.shape
    rows = min(MLP_ROWS, m)
    row = lambda i: (i, 0)
    return pl.pallas_call(
        functools.partial(_mlp_kernel, final_norm=final_norm),
        grid=(m // rows,),
        in_specs=[pl.BlockSpec((rows, d), row), _resident((1, d)), _resident(wu.shape),
                  _resident(wd.shape), _resident((1, d))],
        out_specs=pl.BlockSpec((rows, d), row),
        out_shape=jax.ShapeDtypeStruct((m, d), jnp.float32),
        compiler_params=pltpu.CompilerParams(
            dimension_semantics=("arbitrary",), vmem_limit_bytes=VMEM_LIMIT),
        name="mlp",
    )(x2, g, wu, wd, fg)


def kernel(x, mix_norm_g, w_in, sgu_norm_g, sgu_norm_b, w_spatial, b_spatial, lower_bounds,
           hgrn_norm_g, w_branch_a, w_branch_b, w_out, mlp_norm_g, w_mlp_up, w_mlp_down,
           final_norm_g):
    batch, seq, d = x.shape
    depth = w_in.shape[0]
    assert d == GROUPS * HEAD and seq % MIXER_ROWS == 0 and w_in.shape[2] == N_BRANCH * d

    lb = jnp.cumsum(jax.nn.softmax(_f32(lower_bounds), axis=0), axis=0)
    lb = lb - lb[0:1]
    loglb = jnp.log(lb)
    l1mlb = jnp.log1p(-lb)

    tri = jnp.asarray(_cumsum_matrix(CHUNK), jnp.bfloat16)
    masks = jnp.asarray(_level_masks(CHUNK), jnp.float32)

    row = lambda p, l: p[l].reshape(1, -1)
    x2 = x.reshape(batch * seq, d)
    for l in range(depth):
        zb, lf = _in_proj(x2, row(mix_norm_g, l), _bf16(w_in[l]), row(sgu_norm_g, l),
                          row(sgu_norm_b, l), row(loglb, l), row(l1mlb, l), row(1.0 - lb, l))
        x2 = _mixer(x2, zb, lf, tri, masks, w_spatial[l], b_spatial[l].T,
                    row(hgrn_norm_g, l), _bf16(w_branch_a[l]), _bf16(w_branch_b[l]),
                    _bf16(w_out[l]), batch=batch)
        x2 = _mlp(x2, row(mlp_norm_g, l), _bf16(w_mlp_up[l]), _bf16(w_mlp_down[l]),
                  final_norm_g.reshape(1, -1), final_norm=(l == depth - 1))
    return x2.reshape(batch, seq, d)
```

```python
import functools
import math

import jax
import jax.numpy as jnp
import numpy as np
from jax import lax
from jax.experimental import pallas as pl
from jax.experimental.pallas import tpu as pltpu

EPS = 1e-6
GROUPS = 8
HEAD = 128
CHUNK = 128
SUBLANES = 8
N_BRANCH = 8
ROW_BLOCK = 256
IN_PROJ_ROWS = 512
MIXER_ROWS = 256
MLP_ROWS = 512
VMEM_LIMIT = 56 * 1024 * 1024
LOG2E = 1.0 / math.log(2.0)

_NT = (((1,), (1,)), ((), ()))
_TN = (((0,), (0,)), ((), ()))


def _f32(x):
    return x.astype(jnp.float32)


def _bf16(x):
    return x.astype(jnp.bfloat16)


def _sigmoid(x):
    return 0.5 * (1.0 + jnp.tanh(0.5 * x))


def _gelu(x):
    c = math.sqrt(2.0 / math.pi)
    return 0.5 * x * (1.0 + jnp.tanh(c * (x + 0.044715 * (x * x * x))))


def _silu(x):
    return x * _sigmoid(x)


def _neg_abs(x):
    bits = lax.bitcast_convert_type(x, jnp.uint32) | jnp.uint32(0x80000000)
    return lax.bitcast_convert_type(bits, jnp.float32)


def _dot(a, b):
    return jnp.dot(a, b, preferred_element_type=jnp.float32)


def _dot_nt(a, b):
    return lax.dot_general(a, b, _NT, preferred_element_type=jnp.float32)


def _dot_tn(a, b):
    return lax.dot_general(a, b, _TN, preferred_element_type=jnp.float32)


def _resident(shape):
    nd = len(shape)
    return pl.BlockSpec(shape, lambda *_: (0,) * nd, pipeline_mode=pl.Buffered(1))


def _in_proj_kernel(x_ref, g_ref, w_ref, lng_ref, lnb_ref, loglb_ref, l1mlb_ref, omlb_ref,
                    zb_ref, lf_ref, *, d):
    rows = x_ref.shape[0]
    for r0 in range(0, rows, ROW_BLOCK):
        rs = slice(r0, r0 + ROW_BLOCK)
        x = x_ref[rs, :]
        h = x * lax.rsqrt(jnp.mean(x * x, axis=-1, keepdims=True) + EPS) * g_ref[...]
        hb = _bf16(h)
        for c in range(N_BRANCH):
            cs = slice(c * d, (c + 1) * d)
            z = _dot(hb, w_ref[:, cs])
            if c == 0:
                out = _gelu(z)
            elif c == 1:
                gv = _gelu(z)
                mu = jnp.mean(gv, axis=-1, keepdims=True)
                xc = gv - mu
                var = jnp.mean(xc * xc, axis=-1, keepdims=True)
                out = xc * lax.rsqrt(var + EPS) * lng_ref[...] + lnb_ref[...]
            elif c in (2, 5):
                out = _silu(z)
            elif c == 3:
                ez = jnp.exp(-jnp.abs(z))
                log_sig = jnp.minimum(z, 0.0) - jnp.log1p(ez)
                a = l1mlb_ref[...] + log_sig
                b = loglb_ref[...]
                lf = jnp.maximum(a, b) + jnp.log1p(jnp.exp(-jnp.abs(a - b)))
                lf_ref[rs, :] = lf
                out = omlb_ref[...] * (jnp.where(z >= 0.0, ez, 1.0) / (1.0 + ez))
            elif c == 4:
                out = z
            else:
                out = _sigmoid(z)
            zb_ref[rs, cs] = _bf16(out)


def _in_proj(x2, g, w, lng, lnb, loglb, l1mlb, omlb):
    m, d = x2.shape
    n = w.shape[1]
    rows = min(IN_PROJ_ROWS, m)
    row = lambda i: (i, 0)
    return pl.pallas_call(
        functools.partial(_in_proj_kernel, d=d),
        grid=(m // rows,),
        in_specs=[
            pl.BlockSpec((rows, d), row),
            _resident((1, d)), _resident((d, n)), _resident((1, d)), _resident((1, d)),
            _resident((1, d)), _resident((1, d)), _resident((1, d)),
        ],
        out_specs=[pl.BlockSpec((rows, n), row), pl.BlockSpec((rows, d), row)],
        out_shape=[jax.ShapeDtypeStruct((m, n), jnp.bfloat16),
                   jax.ShapeDtypeStruct((m, d), jnp.float32)],
        compiler_params=pltpu.CompilerParams(
            dimension_semantics=("arbitrary",), vmem_limit_bytes=VMEM_LIMIT),
        name="in_proj",
    )(x2, g, w, lng, lnb, loglb, l1mlb, omlb)


def _level_halves(t):
    h = t // 2
    out = []
    while h >= 1:
        out.append(h)
        h //= 2
    return out


def _cumsum_matrix(t):
    tri = np.tril(np.ones((t, t), np.float32))
    return np.concatenate([tri, tri], axis=1)


def _level_masks(t):
    idx = np.arange(t)
    masks = []
    for h in _level_halves(t):
        blk = idx // (2 * h)
        second = (idx % (2 * h)) >= h
        masks.append((blk[:, None] == blk[None, :]) & second[:, None] & ~second[None, :])
    return np.stack(masks).astype(np.float32)


def _mixer_kernel(x_ref, zb_ref, lf_ref, tri_ref, mask_ref, ws_ref, bst_ref, ng_ref,
                  wa_ref, wb_ref, wo_ref, out_ref,
                  state_ref, wsm_ref, bc_ref, qf_ref, kf_ref, o_ref, pa_ref, a_ref, b_ref,
                  *, d):
    rows = x_ref.shape[0]
    t = CHUNK
    halves = _level_halves(t)

    @pl.when((pl.program_id(0) == 0) & (pl.program_id(1) == 0))
    def _():
        tok = lax.broadcasted_iota(jnp.int32, (t, t), 0)
        src = lax.broadcasted_iota(jnp.int32, (t, t), 1)
        for g in range(GROUPS):
            wsm_ref[g] = _bf16(jnp.where(tok >= src, ws_ref[g], 0.0))

    @pl.when(pl.program_id(1) == 0)
    def _():
        state_ref[...] = jnp.zeros_like(state_ref)

    for r0 in range(0, rows, t):
        lf = lf_ref[r0:r0 + t, :]
        hi = _bf16(lf)
        lo = _bf16(lf - _f32(hi))
        bc_ref[r0:r0 + t, :] = LOG2E * _dot(tri_ref[...], jnp.concatenate([hi, lo], axis=0))

    qf_ref[...] = _f32(zb_ref[:, 2 * d:3 * d])
    kf_ref[...] = _f32(zb_ref[:, 3 * d:4 * d])

    sub = lax.broadcasted_iota(jnp.int32, (t, HEAD), 0)
    eye =(lax.broadcasted_iota(jnp.int32, (t, t), 0)
           == lax.broadcasted_iota(jnp.int32, (t, t), 1))

    def bcast_row(r, n, hs):
        return jnp.broadcast_to(bc_ref[r:r + 1, hs], (n, HEAD))

    def zcol(r0, c, g):
        lo_ = c * d + g * HEAD
        return zb_ref[r0:r0 + t, lo_:lo_ + HEAD]

    for r0 in range(0, rows, t):
        for g in range(GROUPS):
            mixed = _dot(wsm_ref[g], zcol(r0, 1, g)) + bst_ref[:, g:g + 1]
            a_ref[r0:r0 + t, g * HEAD:(g + 1) * HEAD] = _bf16(_f32(zcol(r0, 0, g)) * mixed)

    pa_ref[...] = _dot(a_ref[...], wa_ref[...])

    units = [(r0, g) for r0 in range(0, rows, t) for g in range(GROUPS)]
    srows = {u: [None] * (t // SUBLANES) for u in units}

    def add_rows(u, row0, s):
        for j in range(s.shape[0] // SUBLANES):
            i = row0 // SUBLANES + j
            piece = s[j * SUBLANES:(j + 1) * SUBLANES]
            srows[u][i] = piece if srows[u][i] is None else srows[u][i] + piece

    def level_scores(u, lev, h):
        r0, g = u
        hs = slice(g * HEAD, (g + 1) * HEAD)
        bc = bc_ref[r0:r0 + t, hs]
        n_blk = t // (2 * h)
        if h >= SUBLANES:
            q = qf_ref[r0:r0 + t, hs]
            k = kf_ref[r0:r0 + t, hs]
            q_parts, k_parts = [], []
            for blk in range(n_blk):
                b0 = blk * 2 * h
                ref = bcast_row(r0 + b0 + h - 1, h, hs)
                q_parts.append(q[b0 + h:b0 + 2 * h] * jnp.exp2(bc[b0 + h:b0 + 2 * h] - ref))
                k_parts.append(k[b0:b0 + h] * jnp.exp2(ref - bc[b0:b0 + h]))
                k_parts.append(jnp.zeros((h, HEAD), jnp.float32))
            s = _dot_nt(_bf16(jnp.concatenate(q_parts, axis=0)),
                        _bf16(jnp.concatenate(k_parts, axis=0)))
            if n_blk > 1:
                s = s * jnp.concatenate(
                    [mask_ref[lev, blk * 2 * h + h:(blk + 1) * 2 * h, :]
                     for blk in range(n_blk)], axis=0)
            for blk in range(n_blk):
                add_rows(u, blk * 2 * h + h, s[blk * h:(blk + 1) * h])
        else:
            if h == 1:
                nabs = jnp.where((sub & 1) == 1, LOG2E * lf_ref[r0:r0 + t, hs], 0.0)
            else:
                parts = []
                for v0 in range(0, t, SUBLANES):
                    if 2 * h == SUBLANES:
                        parts.append(bcast_row(r0 + v0 + h - 1, SUBLANES, hs))
                    else:
                        first = bcast_row(r0 + v0 + h - 1, SUBLANES, hs)
                        second = bcast_row(r0 + v0 + 2 * h + h - 1, SUBLANES, hs)
                        parts.append(jnp.where((sub[:SUBLANES] & (2 * h)) == 0, first, second))
                nabs = _neg_abs(bc - jnp.concatenate(parts, axis=0))
            e = jnp.exp2(nabs)
            s = _dot_nt(_bf16(qf_ref[r0:r0 + t, hs] * e), _bf16(kf_ref[r0:r0 + t, hs] * e))
            add_rows(u, 0, s * mask_ref[lev])

    for lev, h in enumerate(halves):
        for u in units:
            level_scores(u, lev, h)

    for u in units:
        r0, g = u
        hs = slice(g * HEAD, (g + 1) * HEAD)
        q = qf_ref[r0:r0 + t, hs]
        k = kf_ref[r0:r0 + t, hs]
        vb = zcol(r0, 4, g)
        bc = bc_ref[r0:r0 + t, hs]
        diag = jnp.sum(q * k, axis=-1, keepdims=True)
        scores = jnp.where(eye, diag, jnp.concatenate(srows[u], axis=0))
        st = state_ref[g]
        q_in = _bf16(q * jnp.exp2(bc))
        o_ref[r0:r0 + t, hs] = _dot(jnp.concatenate([_bf16(scores), q_in], axis=1),
                                    jnp.concatenate([vb, _bf16(st)], axis=0))
        b_last = bcast_row(r0 + t - 1, t, hs)
        k_end = _bf16(k * jnp.exp2(b_last - bc))
        state_ref[g] = st * jnp.transpose(jnp.exp2(b_last)) + _dot_tn(k_end, vb)

    for g in range(GROUPS):
        hs = slice(g * HEAD, (g + 1) * HEAD)
        o = o_ref[:, hs]
        o = o * lax.rsqrt(jnp.mean(o * o, axis=-1, keepdims=True) + EPS) * ng_ref[...]
        b_ref[:, hs] = _bf16(o * _f32(zb_ref[:, 5 * d + g * HEAD:5 * d + (g + 1) * HEAD]))

    pb = _dot(b_ref[...], wb_ref[...])
    merged = _f32(zb_ref[:, 6 * d:7 * d]) * pa_ref[...] + _f32(zb_ref[:, 7 * d:8 * d]) * pb
    out_ref[...] = x_ref[...] + _dot(_bf16(merged), wo_ref[...])


def _mixer(x2, zb, lf, tri, masks, ws, bst, ng, wa, wb, wo, *, batch):
    m, d = x2.shape
    n = zb.shape[1]
    rows = MIXER_ROWS
    steps = m // batch // rows
    row = lambda b, s: (b * steps + s, 0)
    return pl.pallas_call(
        functools.partial(_mixer_kernel, d=d),
        grid=(batch, steps),
        in_specs=[
            pl.BlockSpec((rows, d), row), pl.BlockSpec((rows, n), row),
            pl.BlockSpec((rows, d), row),
            _resident(tri.shape), _resident(masks.shape), _resident(ws.shape),
            _resident(bst.shape), _resident(ng.shape),
            _resident(wa.shape), _resident(wb.shape), _resident(wo.shape),
        ],
        out_specs=pl.BlockSpec((rows, d), row),
        out_shape=jax.ShapeDtypeStruct((m, d), jnp.float32),
        scratch_shapes=[
            pltpu.VMEM((GROUPS, HEAD, HEAD), jnp.float32),
            pltpu.VMEM((GROUPS, CHUNK, CHUNK), jnp.bfloat16),
            pltpu.VMEM((rows, d), jnp.float32),
            pltpu.VMEM((rows, d), jnp.float32),
            pltpu.VMEM((rows, d), jnp.float32),
            pltpu.VMEM((rows, d), jnp.float32),
            pltpu.VMEM((rows, d), jnp.float32),
            pltpu.VMEM((rows, d), jnp.bfloat16),
            pltpu.VMEM((rows, d), jnp.bfloat16),
        ],
        compiler_params=pltpu.CompilerParams(
            dimension_semantics=("arbitrary", "arbitrary"), vmem_limit_bytes=VMEM_LIMIT),
        name="mixer",
    )(x2, zb, lf, tri, masks, ws, bst, ng, wa, wb, wo)


def _mlp_kernel(x_ref, g_ref, wu_ref, wd_ref, fg_ref, out_ref, *, final_norm):
    rows = x_ref.shape[0]
    for r0 in range(0, rows, ROW_BLOCK):
        rs = slice(r0, r0 + ROW_BLOCK)
        x = x_ref[rs, :]
        h = x * lax.rsqrt(jnp.mean(x * x, axis=-1, keepdims=True) + EPS) * g_ref[...]
        up = jnp.maximum(_dot(_bf16(h), wu_ref[...]), 0.0)
        y = x + _dot(_bf16(up * up), wd_ref[...])
        if final_norm:
            y = y * lax.rsqrt(jnp.mean(y * y, axis=-1, keepdims=True) + EPS) * fg_ref[...]
        out_ref[rs, :] = y


def _mlp(x2, g, wu, wd, fg, *, final_norm):
    m, d = x2.shape
    rows = min(MLP_ROWS, m)
    row = lambda i: (i, 0)
    return pl.pallas_call(
        functools.partial(_mlp_kernel, final_norm=final_norm),
        grid=(m // rows,),
        in_specs=[pl.BlockSpec((rows, d), row), _resident((1, d)), _resident(wu.shape),
                  _resident(wd.shape), _resident((1, d))],
        out_specs=pl.BlockSpec((rows, d), row),
        out_shape=jax.ShapeDtypeStruct((m, d), jnp.float32),
        compiler_params=pltpu.CompilerParams(
            dimension_semantics=("arbitrary",), vmem_limit_bytes=VMEM_LIMIT),
        name="mlp",
    )(x2, g, wu, wd, fg)


def kernel(x, mix_norm_g, w_in, sgu_norm_g, sgu_norm_b, w_spatial, b_spatial, lower_bounds,
           hgrn_norm_g, w_branch_a, w_branch_b, w_out, mlp_norm_g, w_mlp_up, w_mlp_down,
           final_norm_g):
    batch, seq, d = x.shape
    depth = w_in.shape[0]
    assert d == GROUPS * HEAD and seq % MIXER_ROWS == 0 and w_in.shape[2] == N_BRANCH * d

    lb = jnp.cumsum(jax.nn.softmax(_f32(lower_bounds), axis=0), axis=0)
    lb = lb - lb[0:1]
    loglb = jnp.log(lb)
    l1mlb = jnp.log1p(-lb)

    tri = jnp.asarray(_cumsum_matrix(CHUNK), jnp.bfloat16)
    masks = jnp.asarray(_level_masks(CHUNK), jnp.float32)

    row = lambda p, l: p[l].reshape(1, -1)
    x2 = x.reshape(batch * seq, d)
    for l in range(depth):
        zb, lf = _in_proj(x2, row(mix_norm_g, l), _bf16(w_in[l]), row(sgu_norm_g, l),
                          row(sgu_norm_b, l), row(loglb, l), row(l1mlb, l), row(1.0 - lb, l))
        x2 = _mixer(x2, zb, lf, tri, masks, w_spatial[l], b_spatial[l].T,
                    row(hgrn_norm_g, l), _bf16(w_branch_a[l]), _bf16(w_branch_b[l]),
                    _bf16(w_out[l]), batch=batch)
        x2 = _mlp(x2, row(mlp_norm_g, l), _bf16(w_mlp_up[l]), _bf16(w_mlp_down[l]),
                  final_norm_g.reshape(1, -1), final_norm=(l == depth - 1))
    return x2.reshape(batch, seq, d)
```

```python
import functools
import math

import jax
import jax.numpy as jnp
import numpy as np
from jax import lax
from jax.experimental import pallas as pl
from jax.experimental.pallas import tpu as pltpu

EPS = 1e-6
GROUPS = 8
HEAD = 128
CHUNK = 128
SUBLANES = 8
N_BRANCH = 8
ROW_BLOCK = 256
IN_PROJ_BLOCK = 256
IN_PROJ_COLS = 512
BRANCH_ORDER = (4, 3, 6, 1, 7, 0, 2, 5)
IN_PROJ_ROWS = 512
MIXER_ROWS = 256
MLP_ROWS = 512
VMEM_LIMIT = 56 * 1024 * 1024
LOG2E = 1.0 / math.log(2.0)

_NT = (((1,), (1,)), ((), ()))
_TN = (((0,), (0,)), ((), ()))


def _f32(x):
    return x.astype(jnp.float32)


def _bf16(x):
    return x.astype(jnp.bfloat16)


def _sigmoid(x):
    return 0.5 * (1.0 + jnp.tanh(0.5 * x))


def _gelu(x):
    c = math.sqrt(2.0 / math.pi)
    return 0.5 * x * (1.0 + jnp.tanh(c * (x + 0.044715 * (x * x * x))))


def _silu(x):
    return x * _sigmoid(x)


def _neg_abs(x):
    bits = lax.bitcast_convert_type(x, jnp.uint32) | jnp.uint32(0x80000000)
    return lax.bitcast_convert_type(bits, jnp.float32)


def _dot(a, b):
    return jnp.dot(a, b, preferred_element_type=jnp.float32)


def _dot_nt(a, b):
    return lax.dot_general(a, b, _NT, preferred_element_type=jnp.float32)


def _dot_tn(a, b):
    return lax.dot_general(a, b, _TN, preferred_element_type=jnp.float32)


def _resident(shape):
    nd = len(shape)
    return pl.BlockSpec(shape, lambda *_: (0,) * nd, pipeline_mode=pl.Buffered(1))


def _in_proj_kernel(x_ref, g_ref, w_ref, lng_ref, lnb_ref, lb_ref, lguard_ref, omlb_ref,
                    zb_ref, lf_ref, hb_ref, *, d):
    rows = x_ref.shape[0]
    blocks = []
    for r0 in range(0, rows, IN_PROJ_BLOCK):
        for c in BRANCH_ORDER:
            width = d if c == 1 else IN_PROJ_COLS
            blocks += [(r0, c, c0, width) for c0 in range(0, d, width)]

    def matmul(r0, c, c0, width):
        rs = slice(r0, r0 + IN_PROJ_BLOCK)
        if c == BRANCH_ORDER[0] and c0 == 0:
            x = x_ref[rs, :]
            h = x * lax.rsqrt(jnp.mean(x * x, axis=-1, keepdims=True) + EPS) * g_ref[...]
            hb_ref[rs, :] = _bf16(h)
        return _dot(hb_ref[rs, :], w_ref[:, c * d + c0:c * d + c0 + width])

    def epilogue(r0, c, c0, width, z):
        rs = slice(r0, r0 + IN_PROJ_BLOCK)
        cs = slice(c0, c0 + width)
        if c == 0:
            out = _gelu(z)
        elif c == 1:
            gv = _gelu(z)
            mu = jnp.mean(gv, axis=-1, keepdims=True)
            xc = gv - mu
            var = jnp.mean(xc * xc, axis=-1, keepdims=True)
            out = xc * lax.rsqrt(var + EPS) * lng_ref[...] + lnb_ref[...]
        elif c in (2, 5):
            out = _silu(z)
        elif c == 3:
            ez = jnp.exp(_neg_abs(z))
            r = 1.0 / (1.0 + ez)
            er = ez * r
            pos = z >= 0.0
            f = lb_ref[:, cs] + omlb_ref[:, cs] * jnp.where(pos, r, er)
            lf_ref[rs, cs] = jnp.maximum(jnp.log(f), lguard_ref[:, cs] + jnp.minimum(z, 0.0))
            out = omlb_ref[:, cs] * jnp.where(pos, er, r)
        elif c == 4:
            out = z
        else:
            out = _sigmoid(z)
        zb_ref[rs, c * d + c0:c * d + c0 + width] = _bf16(out)

    z = matmul(*blocks[0])
    for i, blk in enumerate(blocks):
        z_next = matmul(*blocks[i + 1]) if i + 1 < len(blocks) else None
        epilogue(*blk, z)
        z = z_next


def _in_proj(x2, g, w, lng, lnb, lb, lguard, omlb):
    m, d = x2.shape
    n = w.shape[1]
    rows = min(IN_PROJ_ROWS, m)
    row = lambda i: (i, 0)
    return pl.pallas_call(
        functools.partial(_in_proj_kernel, d=d),
        grid=(m // rows,),
        in_specs=[
            pl.BlockSpec((rows, d), row),
            _resident((1, d)), _resident((d, n)), _resident((1, d)), _resident((1, d)),
            _resident((1, d)), _resident((1, d)), _resident((1, d)),
        ],
        out_specs=[pl.BlockSpec((rows, n), row), pl.BlockSpec((rows, d), row)],
        out_shape=[jax.ShapeDtypeStruct((m, n), jnp.bfloat16),
                   jax.ShapeDtypeStruct((m, d), jnp.float32)],
        scratch_shapes=[pltpu.VMEM((rows, d), jnp.bfloat16)],
        compiler_params=pltpu.CompilerParams(
            dimension_semantics=("arbitrary",), vmem_limit_bytes=VMEM_LIMIT),
        name="in_proj",
    )(x2, g, w, lng, lnb, lb, lguard, omlb)


def _level_halves(t):
    h = t // 2
    out = []
    while h >= 1:
        out.append(h)
        h //= 2
    return out


def _cumsum_matrix(t):
    tri = np.tril(np.ones((t, t), np.float32))
    return np.concatenate([tri, tri], axis=1)


def _level_masks(t):
    idx = np.arange(t)
    masks = []
    for h in _level_halves(t):
        blk = idx // (2 * h)
        second = (idx % (2 * h)) >= h
        masks.append((blk[:, None] == blk[None, :]) & second[:, None] & ~second[None, :])
    return np.stack(masks).astype(np.float32)


def _mixer_kernel(x_ref, zb_ref, lf_ref, tri_ref, mask_ref, ws_ref, bst_ref, ng_ref,
                  wa_ref, wb_ref, wo_ref, out_ref,
                  state_ref, wsm_ref, bc_ref, qf_ref, kf_ref, o_ref, pa_ref, a_ref, b_ref,
                  *, d):
    rows = x_ref.shape[0]
    t = CHUNK
    halves = _level_halves(t)

    @pl.when((pl.program_id(0) == 0) & (pl.program_id(1) == 0))
    def _():
        tok = lax.broadcasted_iota(jnp.int32, (t, t), 0)
        src = lax.broadcasted_iota(jnp.int32, (t, t), 1)
        for g in range(GROUPS):
            wsm_ref[g] = _bf16(jnp.where(tok >= src, ws_ref[g], 0.0))

    @pl.when(pl.program_id(1) == 0)
    def _():
        state_ref[...] = jnp.zeros_like(state_ref)

    for r0 in range(0, rows, t):
        lf = lf_ref[r0:r0 + t, :]
        hi = _bf16(lf)
        lo = _bf16(lf - _f32(hi))
        bc_ref[r0:r0 + t, :] = LOG2E * _dot(tri_ref[...], jnp.concatenate([hi, lo], axis=0))

    qf_ref[...] = _f32(zb_ref[:, 2 * d:3 * d])
    kf_ref[...] = _f32(zb_ref[:, 3 * d:4 * d])

    sub = lax.broadcasted_iota(jnp.int32, (t, HEAD), 0)
    eye =(lax.broadcasted_iota(jnp.int32, (t, t), 0)
           == lax.broadcasted_iota(jnp.int32, (t, t), 1))

    def bcast_row(r, n, hs):
        return jnp.broadcast_to(bc_ref[r:r + 1, hs], (n, HEAD))

    def zcol(r0, c, g):
        lo_ = c * d + g * HEAD
        return zb_ref[r0:r0 + t, lo_:lo_ + HEAD]

    for r0 in range(0, rows, t):
        for g in range(GROUPS):
            mixed = _dot(wsm_ref[g], zcol(r0, 1, g)) + bst_ref[:, g:g + 1]
            a_ref[r0:r0 + t, g * HEAD:(g + 1) * HEAD] = _bf16(_f32(zcol(r0, 0, g)) * mixed)

    pa_ref[...] = _dot(a_ref[...], wa_ref[...])

    units = [(r0, g) for r0 in range(0, rows, t) for g in range(GROUPS)]
    srows = {u: [None] * (t // SUBLANES) for u in units}

    def add_rows(u, row0, s):
        for j in range(s.shape[0] // SUBLANES):
            i = row0 // SUBLANES + j
            piece = s[j * SUBLANES:(j + 1) * SUBLANES]
            srows[u][i] = piece if srows[u][i] is None else srows[u][i] + piece

    def level_scores(u, lev, h):
        r0, g = u
        hs = slice(g * HEAD, (g + 1) * HEAD)
        bc = bc_ref[r0:r0 + t, hs]
        n_blk = t // (2 * h)
        if h >= SUBLANES:
            q = qf_ref[r0:r0 + t, hs]
            k = kf_ref[r0:r0 + t, hs]
            q_parts, k_parts = [], []
            for blk in range(n_blk):
                b0 = blk * 2 * h
                ref = bcast_row(r0 + b0 + h - 1, h, hs)
                q_parts.append(q[b0 + h:b0 + 2 * h] * jnp.exp2(bc[b0 + h:b0 + 2 * h] - ref))
                k_parts.append(k[b0:b0 + h] * jnp.exp2(ref - bc[b0:b0 + h]))
                k_parts.append(jnp.zeros((h, HEAD), jnp.float32))
            s = _dot_nt(_bf16(jnp.concatenate(q_parts, axis=0)),
                        _bf16(jnp.concatenate(k_parts, axis=0)))
            if n_blk > 1:
                s = s * jnp.concatenate(
                    [mask_ref[lev, blk * 2 * h + h:(blk + 1) * 2 * h, :]
                     for blk in range(n_blk)], axis=0)
            for blk in range(n_blk):
                add_rows(u, blk * 2 * h + h, s[blk * h:(blk + 1) * h])
        else:
            if h == 1:
                nabs = jnp.where((sub & 1) == 1, LOG2E * lf_ref[r0:r0 + t, hs], 0.0)
            else:
                parts = []
                for v0 in range(0, t, SUBLANES):
                    if 2 * h == SUBLANES:
                        parts.append(bcast_row(r0 + v0 + h - 1, SUBLANES, hs))
                    else:
                        first = bcast_row(r0 + v0 + h - 1, SUBLANES, hs)
                        second = bcast_row(r0 + v0 + 2 * h + h - 1, SUBLANES, hs)
                        parts.append(jnp.where((sub[:SUBLANES] & (2 * h)) == 0, first, second))
                nabs = _neg_abs(bc - jnp.concatenate(parts, axis=0))
            e = jnp.exp2(nabs)
            s = _dot_nt(_bf16(qf_ref[r0:r0 + t, hs] * e), _bf16(kf_ref[r0:r0 + t, hs] * e))
            add_rows(u, 0, s * mask_ref[lev])

    for lev, h in enumerate(halves):
        for u in units:
            level_scores(u, lev, h)

    for u in units:
        r0, g = u
        hs = slice(g * HEAD, (g + 1) * HEAD)
        q = qf_ref[r0:r0 + t, hs]
        k = kf_ref[r0:r0 + t, hs]
        vb = zcol(r0, 4, g)
        bc = bc_ref[r0:r0 + t, hs]
        diag = jnp.sum(q * k, axis=-1, keepdims=True)
        scores = jnp.where(eye, diag, jnp.concatenate(srows[u], axis=0))
        st = state_ref[g]
        q_in = _bf16(q * jnp.exp2(bc))
        o_ref[r0:r0 + t, hs] = _dot(jnp.concatenate([_bf16(scores), q_in], axis=1),
                                    jnp.concatenate([vb, _bf16(st)], axis=0))
        b_last = bcast_row(r0 + t - 1, t, hs)
        k_end = _bf16(k * jnp.exp2(b_last - bc))
        state_ref[g] = st * jnp.transpose(jnp.exp2(b_last)) + _dot_tn(k_end, vb)

    for g in range(GROUPS):
        hs = slice(g * HEAD, (g + 1) * HEAD)
        o = o_ref[:, hs]
        o = o * lax.rsqrt(jnp.mean(o * o, axis=-1, keepdims=True) + EPS) * ng_ref[...]
        b_ref[:, hs] = _bf16(o * _f32(zb_ref[:, 5 * d + g * HEAD:5 * d + (g + 1) * HEAD]))

    pb = _dot(b_ref[...], wb_ref[...])
    merged = _f32(zb_ref[:, 6 * d:7 * d]) * pa_ref[...] + _f32(zb_ref[:, 7 * d:8 * d]) * pb
    out_ref[...] = x_ref[...] + _dot(_bf16(merged), wo_ref[...])


def _mixer(x2, zb, lf, tri, masks, ws, bst, ng, wa, wb, wo, *, batch):
    m, d = x2.shape
    n = zb.shape[1]
    rows = MIXER_ROWS
    steps = m // batch // rows
    row = lambda b, s: (b * steps + s, 0)
    return pl.pallas_call(
        functools.partial(_mixer_kernel, d=d),
        grid=(batch, steps),
        in_specs=[
            pl.BlockSpec((rows, d), row), pl.BlockSpec((rows, n), row),
            pl.BlockSpec((rows, d), row),
            _resident(tri.shape), _resident(masks.shape), _resident(ws.shape),
            _resident(bst.shape), _resident(ng.shape),
            _resident(wa.shape), _resident(wb.shape), _resident(wo.shape),
        ],
        out_specs=pl.BlockSpec((rows, d), row),
        out_shape=jax.ShapeDtypeStruct((m, d), jnp.float32),
        scratch_shapes=[
            pltpu.VMEM((GROUPS, HEAD, HEAD), jnp.float32),
            pltpu.VMEM((GROUPS, CHUNK, CHUNK), jnp.bfloat16),
            pltpu.VMEM((rows, d), jnp.float32),
            pltpu.VMEM((rows, d), jnp.float32),
            pltpu.VMEM((rows, d), jnp.float32),
            pltpu.VMEM((rows, d), jnp.float32),
            pltpu.VMEM((rows, d), jnp.float32),
            pltpu.VMEM((rows, d), jnp.bfloat16),
            pltpu.VMEM((rows, d), jnp.bfloat16),
        ],
        compiler_params=pltpu.CompilerParams(
            dimension_semantics=("arbitrary", "arbitrary"), vmem_limit_bytes=VMEM_LIMIT),
        name="mixer",
    )(x2, zb, lf, tri, masks, ws, bst, ng, wa, wb, wo)


def _mlp_kernel(x_ref, g_ref, wu_ref, wd_ref, fg_ref, out_ref, *, final_norm):
    rows = x_ref.shape[0]
    for r0 in range(0, rows, ROW_BLOCK):
        rs = slice(r0, r0 + ROW_BLOCK)
        x = x_ref[rs, :]
        h = x * lax.rsqrt(jnp.mean(x * x, axis=-1, keepdims=True) + EPS) * g_ref[...]
        up = jnp.maximum(_dot(_bf16(h), wu_ref[...]), 0.0)
        y = x + _dot(_bf16(up * up), wd_ref[...])
        if final_norm:
            y = y * lax.rsqrt(jnp.mean(y * y, axis=-1, keepdims=True) + EPS) * fg_ref[...]
        out_ref[rs, :] = y


def _mlp(x2, g, wu, wd, fg, *, final_norm):
    m, d = x2.shape
    rows = min(MLP_ROWS, m)
    row = lambda i: (i, 0)
    return pl.pallas_call(
        functools.partial(_mlp_kernel, final_norm=final_norm),
        grid=(m // rows,),
        in_specs=[pl.BlockSpec((rows, d), row), _resident((1, d)), _resident(wu.shape),
                  _resident(wd.shape), _resident((1, d))],
        out_specs=pl.BlockSpec((rows, d), row),
        out_shape=jax.ShapeDtypeStruct((m, d), jnp.float32),
        compiler_params=pltpu.CompilerParams(
            dimension_semantics=("arbitrary",), vmem_limit_bytes=VMEM_LIMIT),
        name="mlp",
    )(x2, g, wu, wd, fg)


def kernel(x, mix_norm_g, w_in, sgu_norm_g, sgu_norm_b, w_spatial, b_spatial, lower_bounds,
           hgrn_norm_g, w_branch_a, w_branch_b, w_out, mlp_norm_g, w_mlp_up, w_mlp_down,
           final_norm_g):
    batch, seq, d = x.shape
    depth = w_in.shape[0]
    assert d == GROUPS * HEAD and seq % MIXER_ROWS == 0 and w_in.shape[2] == N_BRANCH * d

    lb = jnp.cumsum(jax.nn.softmax(_f32(lower_bounds), axis=0), axis=0)
    lb = lb - lb[0:1]
    lguard = jnp.log1p(-lb) - math.log(2.0)

    tri = jnp.asarray(_cumsum_matrix(CHUNK), jnp.bfloat16)
    masks = jnp.asarray(_level_masks(CHUNK), jnp.float32)

    row = lambda p, l: p[l].reshape(1, -1)
    x2 = x.reshape(batch * seq, d)
    for l in range(depth):
        zb, lf = _in_proj(x2, row(mix_norm_g, l), _bf16(w_in[l]), row(sgu_norm_g, l),
                          row(sgu_norm_b, l), row(lb, l), row(lguard, l), row(1.0 - lb, l))
        x2 = _mixer(x2, zb, lf, tri, masks, w_spatial[l], b_spatial[l].T,
                    row(hgrn_norm_g, l), _bf16(w_branch_a[l]), _bf16(w_branch_b[l]),
                    _bf16(w_out[l]), batch=batch)
        x2 = _mlp(x2, row(mlp_norm_g, l), _bf16(w_mlp_up[l]), _bf16(w_mlp_down[l]),
                  final_norm_g.reshape(1, -1), final_norm=(l == depth - 1))
    return x2.reshape(batch, seq, d)
```

```python
import functools
import math

import jax
import jax.numpy as jnp
import numpy as np
from jax import lax
from jax.experimental import pallas as pl
from jax.experimental.pallas import tpu as pltpu

EPS = 1e-6
GROUPS = 8
HEAD = 128
CHUNK = 128
SUBLANES = 8
N_BRANCH = 8
ROW_BLOCK = 256
IN_PROJ_COLS = 512
MIXER_ROWS = 256
MLP_ROWS = 512
VMEM_LIMIT = 56 * 1024 * 1024
LOG2E = 1.0 / math.log(2.0)

U, V, Q, K, I, G, GATE_A, GATE_B = range(N_BRANCH)
BRANCH_ORDER = (I, K, GATE_A, V, GATE_B, U, Q, G)
BF16_SLOT = {V: 0, I: 1}
F32_SLOT = {U: 0, Q: 1, K: 2, G: 3, GATE_A: 4, GATE_B: 5}

_NT = (((1,), (1,)), ((), ()))
_TN = (((0,), (0,)), ((), ()))


def _f32(x):
    return x.astype(jnp.float32)


def _bf16(x):
    return x.astype(jnp.bfloat16)


def _sigmoid(x):
    return 0.5 * (1.0 + jnp.tanh(0.5 * x))


def _gelu(x):
    c = math.sqrt(2.0 / math.pi)
    return 0.5 * x * (1.0 + jnp.tanh(c * (x + 0.044715 * (x * x * x))))


def _silu(x):
    return x * _sigmoid(x)


def _neg_abs(x):
    bits = lax.bitcast_convert_type(x, jnp.uint32) | jnp.uint32(0x80000000)
    return lax.bitcast_convert_type(bits, jnp.float32)


def _dot(a, b):
    return jnp.dot(a, b, preferred_element_type=jnp.float32)


def _dot_nt(a, b):
    return lax.dot_general(a, b, _NT, preferred_element_type=jnp.float32)


def _dot_tn(a, b):
    return lax.dot_general(a, b, _TN, preferred_element_type=jnp.float32)


def _resident(shape):
    nd = len(shape)
    return pl.BlockSpec(shape, lambda *_: (0,) * nd, pipeline_mode=pl.Buffered(1))


def _level_halves(t):
    h = t // 2
    out = []
    while h >= 1:
        out.append(h)
        h //= 2
    return out


def _cumsum_matrix(t):
    tri = np.tril(np.ones((t, t), np.float32))
    return np.concatenate([tri, tri], axis=1)


def _level_masks(t):
    idx = np.arange(t)
    masks = []
    for h in _level_halves(t):
        blk = idx // (2 * h)
        second = (idx % (2 * h)) >= h
        masks.append((blk[:, None] == blk[None, :]) & second[:, None] & ~second[None, :])
    return np.stack(masks).astype(np.float32)


def _project_inputs(x_ref, g_ref, w_ref, lng_ref, lnb_ref, lb_ref, lguard_ref, omlb_ref,
                    hb_ref, zf_ref, zh_ref, lf_ref, *, d):
    blocks = []
    for c in BRANCH_ORDER:
        width = d if c == V else IN_PROJ_COLS
        blocks += [(c, c0, width) for c0 in range(0, d, width)]

    x = x_ref[...]
    h = x * lax.rsqrt(jnp.mean(x * x, axis=-1, keepdims=True) + EPS) * g_ref[...]
    hb_ref[...] = _bf16(h)

    def matmul(c, c0, width):
        return _dot(hb_ref[...], w_ref[:, c * d + c0:c * d + c0 + width])

    def epilogue(c, c0, width, z):
        cs = slice(c0, c0 + width)
        if c == U:
            out = _gelu(z)
        elif c == V:
            gv = _gelu(z)
            mu = jnp.mean(gv, axis=-1, keepdims=True)
            xc = gv - mu
            var = jnp.mean(xc * xc, axis=-1, keepdims=True)
            out = xc * lax.rsqrt(var + EPS) * lng_ref[...] + lnb_ref[...]
        elif c in (Q, G):
            out = _silu(z)
        elif c == K:
            ez = jnp.exp(_neg_abs(z))
            r = 1.0 / (1.0 + ez)
            er = ez * r
            pos = z >= 0.0
            f = lb_ref[:, cs] + omlb_ref[:, cs] * jnp.where(pos, r, er)
            lf_ref[:, cs] = jnp.maximum(jnp.log(f), lguard_ref[:, cs] + jnp.minimum(z, 0.0))
            out = omlb_ref[:, cs] * jnp.where(pos, er, r)
        elif c == I:
            out = z
        else:
            out = _sigmoid(z)
        if c in BF16_SLOT:
            zh_ref[BF16_SLOT[c], :, cs] = _bf16(out)
        else:
            zf_ref[F32_SLOT[c], :, cs] = out

    z = matmul(*blocks[0])
    for i, blk in enumerate(blocks):
        z_next = matmul(*blocks[i + 1]) if i + 1 < len(blocks) else None
        epilogue(*blk, z)
        z = z_next


def _mix_tokens(x_ref, tri_ref, mask_ref, ws_ref, bst_ref, ng_ref, wa_ref, wb_ref, wo_ref,
                out_ref, zf_ref, zh_ref, lf_ref, state_ref, wsm_ref, bc_ref, o_ref, pa_ref,
                a_ref, b_ref):
    rows = x_ref.shape[0]
    t = CHUNK
    halves = _level_halves(t)

    @pl.when((pl.program_id(0) == 0) & (pl.program_id(1) == 0))
    def _():
        tok = lax.broadcasted_iota(jnp.int32, (t, t), 0)
        src = lax.broadcasted_iota(jnp.int32, (t, t), 1)
        for g in range(GROUPS):
            wsm_ref[g] = _bf16(jnp.where(tok >= src, ws_ref[g], 0.0))

    @pl.when(pl.program_id(1) == 0)
    def _():
        state_ref[...] = jnp.zeros_like(state_ref)

    for r0 in range(0, rows, t):
        lf = lf_ref[r0:r0 + t, :]
        hi = _bf16(lf)
        lo = _bf16(lf - _f32(hi))
        bc_ref[r0:r0 + t, :] = LOG2E * _dot(tri_ref[...], jnp.concatenate([hi, lo], axis=0))

    sub = lax.broadcasted_iota(jnp.int32, (t, HEAD), 0)
    eye = (lax.broadcasted_iota(jnp.int32, (t, t), 0)
           == lax.broadcasted_iota(jnp.int32, (t, t), 1))

    def bcast_row(r, n, hs):
        return jnp.broadcast_to(bc_ref[r:r + 1, hs], (n, HEAD))

    def act(c, r0, g):
        hs = slice(g * HEAD, (g + 1) * HEAD)
        if c in BF16_SLOT:
            return zh_ref[BF16_SLOT[c], r0:r0 + t, hs]
        return zf_ref[F32_SLOT[c], r0:r0 + t, hs]

    for r0 in range(0, rows, t):
        for g in range(GROUPS):
            mixed = _dot(wsm_ref[g], act(V, r0, g)) + bst_ref[:, g:g + 1]
            a_ref[r0:r0 + t, g * HEAD:(g + 1) * HEAD] = _bf16(act(U, r0, g) * mixed)

    pa_ref[...] = _dot(a_ref[...], wa_ref[...])

    units = [(r0, g) for r0 in range(0, rows, t) for g in range(GROUPS)]
    srows = {u: [None] * (t // SUBLANES) for u in units}

    def add_rows(u, row0, s):
        for j in range(s.shape[0] // SUBLANES):
            i = row0 // SUBLANES + j
            piece = s[j * SUBLANES:(j + 1) * SUBLANES]
            srows[u][i] = piece if srows[u][i] is None else srows[u][i] + piece

    def level_scores(u, lev, h):
        r0, g = u
        hs = slice(g * HEAD, (g + 1) * HEAD)
        bc = bc_ref[r0:r0 + t, hs]
        q = act(Q, r0, g)
        k = act(K, r0, g)
        n_blk = t // (2 * h)
        if h >= SUBLANES:
            q_parts, k_parts = [], []
            for blk in range(n_blk):
                b0 = blk * 2 * h
                ref = bcast_row(r0 + b0 + h - 1, h, hs)
                q_parts.append(q[b0 + h:b0 + 2 * h] * jnp.exp2(bc[b0 + h:b0 + 2 * h] - ref))
                k_parts.append(k[b0:b0 + h] * jnp.exp2(ref - bc[b0:b0 + h]))
                k_parts.append(jnp.zeros((h, HEAD), jnp.float32))
            s = _dot_nt(_bf16(jnp.concatenate(q_parts, axis=0)),
                        _bf16(jnp.concatenate(k_parts, axis=0)))
            if n_blk > 1:
                s = s * jnp.concatenate(
                    [mask_ref[lev, blk * 2 * h + h:(blk + 1) * 2 * h, :]
                     for blk in range(n_blk)], axis=0)
            for blk in range(n_blk):
                add_rows(u, blk * 2 * h + h, s[blk * h:(blk + 1) * h])
        else:
            if h == 1:
                nabs = jnp.where((sub & 1) == 1, LOG2E * lf_ref[r0:r0 + t, hs], 0.0)
            else:
                parts = []
                for v0 in range(0, t, SUBLANES):
                    if 2 * h == SUBLANES:
                        parts.append(bcast_row(r0 + v0 + h - 1, SUBLANES, hs))
                    else:
                        first = bcast_row(r0 + v0 + h - 1, SUBLANES, hs)
                        second = bcast_row(r0 + v0 + 2 * h + h - 1, SUBLANES, hs)
                        parts.append(jnp.where((sub[:SUBLANES] & (2 * h)) == 0, first, second))
                nabs = _neg_abs(bc - jnp.concatenate(parts, axis=0))
            e = jnp.exp2(nabs)
            add_rows(u, 0, _dot_nt(_bf16(q * e), _bf16(k * e)) * mask_ref[lev])

    for lev, h in enumerate(halves):
        for u in units:
            level_scores(u, lev, h)

    for u in units:
        r0, g = u
        hs = slice(g * HEAD, (g + 1) * HEAD)
        q = act(Q, r0, g)
        k = act(K, r0, g)
        vb = act(I, r0, g)
        bc = bc_ref[r0:r0 + t, hs]
        diag = jnp.sum(q * k, axis=-1, keepdims=True)
        scores = jnp.where(eye, diag, jnp.concatenate(srows[u], axis=0))
        st = state_ref[g]
        q_in = _bf16(q * jnp.exp2(bc))
        o_ref[r0:r0 + t, hs] = _dot(jnp.concatenate([_bf16(scores), q_in], axis=1),
                                    jnp.concatenate([vb, _bf16(st)], axis=0))
        b_last = bcast_row(r0 + t - 1, t, hs)
        k_end = _bf16(k * jnp.exp2(b_last - bc))
        state_ref[g] = st * jnp.transpose(jnp.exp2(b_last)) + _dot_tn(k_end, vb)

    for g in range(GROUPS):
        hs = slice(g * HEAD, (g + 1) * HEAD)
        o = o_ref[:, hs]
        o = o * lax.rsqrt(jnp.mean(o * o, axis=-1, keepdims=True) + EPS) * ng_ref[...]
        b_ref[:, hs] = _bf16(o * zf_ref[F32_SLOT[G], :, hs])

    pb = _dot(b_ref[...], wb_ref[...])
    merged = zf_ref[F32_SLOT[GATE_A]] * pa_ref[...] + zf_ref[F32_SLOT[GATE_B]] * pb
    out_ref[...] = x_ref[...] + _dot(_bf16(merged), wo_ref[...])


def _mixer_kernel(x_ref, g_ref, w_ref, lng_ref, lnb_ref, lb_ref, lguard_ref, omlb_ref,
                  tri_ref, mask_ref, ws_ref, bst_ref, ng_ref, wa_ref, wb_ref, wo_ref,
                  out_ref,
                  hb_ref, zf_ref, zh_ref, lf_ref, state_ref, wsm_ref, bc_ref, o_ref, pa_ref,
                  a_ref, b_ref, *, d):
    _project_inputs(x_ref, g_ref, w_ref, lng_ref, lnb_ref, lb_ref, lguard_ref, omlb_ref,
                    hb_ref, zf_ref, zh_ref, lf_ref, d=d)
    _mix_tokens(x_ref, tri_ref, mask_ref, ws_ref, bst_ref, ng_ref, wa_ref, wb_ref, wo_ref,
                out_ref, zf_ref, zh_ref, lf_ref, state_ref, wsm_ref, bc_ref, o_ref, pa_ref,
                a_ref, b_ref)


def _mixer(x2, g, w, lng, lnb, lb, lguard, omlb, tri, masks, ws, bst, ng, wa, wb, wo, *,
           batch):
    m, d = x2.shape
    rows = MIXER_ROWS
    steps = m // batch // rows
    row = lambda b, s: (b * steps + s, 0)
    vec = _resident((1, d))
    return pl.pallas_call(
        functools.partial(_mixer_kernel, d=d),
        grid=(batch, steps),
        in_specs=[
            pl.BlockSpec((rows, d), row),
            vec, _resident(w.shape), vec, vec, vec, vec, vec,
            _resident(tri.shape), _resident(masks.shape), _resident(ws.shape),
            _resident(bst.shape), _resident(ng.shape),
            _resident(wa.shape), _resident(wb.shape), _resident(wo.shape),
        ],
        out_specs=pl.BlockSpec((rows, d), row),
        out_shape=jax.ShapeDtypeStruct((m, d), jnp.float32),
        scratch_shapes=[
            pltpu.VMEM((rows, d), jnp.bfloat16),
            pltpu.VMEM((len(F32_SLOT), rows, d), jnp.float32),
            pltpu.VMEM((len(BF16_SLOT), rows, d), jnp.bfloat16),
            pltpu.VMEM((rows, d), jnp.float32),
            pltpu.VMEM((GROUPS, HEAD, HEAD), jnp.float32),
            pltpu.VMEM((GROUPS, CHUNK, CHUNK), jnp.bfloat16),
            pltpu.VMEM((rows, d), jnp.float32),
            pltpu.VMEM((rows, d), jnp.float32),
            pltpu.VMEM((rows, d), jnp.float32),
            pltpu.VMEM((rows, d), jnp.bfloat16),
            pltpu.VMEM((rows, d), jnp.bfloat16),
        ],
        compiler_params=pltpu.CompilerParams(
            dimension_semantics=("arbitrary", "arbitrary"), vmem_limit_bytes=VMEM_LIMIT),
        name="mixer",
    )(x2, g, w, lng, lnb, lb, lguard, omlb, tri, masks, ws, bst, ng, wa, wb, wo)


def _mlp_kernel(x_ref, g_ref, wu_ref, wd_ref, fg_ref, out_ref, *, final_norm):
    rows = x_ref.shape[0]
    for r0 in range(0, rows, ROW_BLOCK):
        rs = slice(r0, r0 + ROW_BLOCK)
        x = x_ref[rs, :]
        h = x * lax.rsqrt(jnp.mean(x * x, axis=-1, keepdims=True) + EPS) * g_ref[...]
        up = jnp.maximum(_dot(_bf16(h), wu_ref[...]), 0.0)
        y = x + _dot(_bf16(up * up), wd_ref[...])
        if final_norm:
            y = y * lax.rsqrt(jnp.mean(y * y, axis=-1, keepdims=True) + EPS) * fg_ref[...]
        out_ref[rs, :] = y


def _mlp(x2, g, wu, wd, fg, *, final_norm):
    m, d = x2.shape
    rows = min(MLP_ROWS, m)
    row = lambda i: (i, 0)
    return pl.pallas_call(
        functools.partial(_mlp_kernel, final_norm=final_norm),
        grid=(m // rows,),
        in_specs=[pl.BlockSpec((rows, d), row), _resident((1, d)), _resident(wu.shape),
                  _resident(wd.shape), _resident((1, d))],
        out_specs=pl.BlockSpec((rows, d), row),
        out_shape=jax.ShapeDtypeStruct((m, d), jnp.float32),
        compiler_params=pltpu.CompilerParams(
            dimension_semantics=("arbitrary",), vmem_limit_bytes=VMEM_LIMIT),
        name="mlp",
    )(x2, g, wu, wd, fg)


def kernel(x, mix_norm_g, w_in, sgu_norm_g, sgu_norm_b, w_spatial, b_spatial, lower_bounds,
           hgrn_norm_g, w_branch_a, w_branch_b, w_out, mlp_norm_g, w_mlp_up, w_mlp_down,
           final_norm_g):
    batch, seq, d = x.shape
    depth = w_in.shape[0]
    assert d == GROUPS * HEAD and seq % MIXER_ROWS == 0 and w_in.shape[2] == N_BRANCH * d

    lb = jnp.cumsum(jax.nn.softmax(_f32(lower_bounds), axis=0), axis=0)
    lb = lb - lb[0:1]
    lguard = jnp.log1p(-lb) - math.log(2.0)

    tri = jnp.asarray(_cumsum_matrix(CHUNK), jnp.bfloat16)
    masks = jnp.asarray(_level_masks(CHUNK), jnp.float32)

    row = lambda p, l: p[l].reshape(1, -1)
    x2 = x.reshape(batch * seq, d)
    for l in range(depth):
        x2 = _mixer(x2, row(mix_norm_g, l), _bf16(w_in[l]), row(sgu_norm_g, l),
                    row(sgu_norm_b, l), row(lb, l), row(lguard, l), row(1.0 - lb, l),
                    tri, masks, w_spatial[l], b_spatial[l].T, row(hgrn_norm_g, l),
                    _bf16(w_branch_a[l]), _bf16(w_branch_b[l]), _bf16(w_out[l]), batch=batch)
        x2 = _mlp(x2, row(mlp_norm_g, l), _bf16(w_mlp_up[l]), _bf16(w_mlp_down[l]),
                  final_norm_g.reshape(1, -1), final_norm=(l == depth - 1))
    return x2.reshape(batch, seq, d)
```

```python
import functools
import math

import jax
import jax.numpy as jnp
import numpy as np
from jax import lax
from jax.experimental import pallas as pl
from jax.experimental.pallas import tpu as pltpu

EPS = 1e-6
GROUPS = 8
HEAD = 128
CHUNK = 128
SUBLANES = 8
N_BRANCH = 8
ROW_BLOCK = 256
IN_PROJ_COLS = 512
MIXER_ROWS = 256
MLP_ROWS = 512
VMEM_LIMIT = 56 * 1024 * 1024
LOG2E = 1.0 / math.log(2.0)

U, V, Q, K, I, G, GATE_A, GATE_B = range(N_BRANCH)
BRANCH_ORDER = (I, K, GATE_A, V, GATE_B, U, Q, G)
BF16_SLOT = {V: 0, I: 1}
F32_SLOT = {U: 0, Q: 1, G: 2, GATE_A: 3, GATE_B: 4}

_NT = (((1,), (1,)), ((), ()))
_TN = (((0,), (0,)), ((), ()))


def _f32(x):
    return x.astype(jnp.float32)


def _bf16(x):
    return x.astype(jnp.bfloat16)


def _sigmoid(x):
    return 0.5 * (1.0 + jnp.tanh(0.5 * x))


def _gelu(x):
    c = math.sqrt(2.0 / math.pi)
    return 0.5 * x * (1.0 + jnp.tanh(c * (x + 0.044715 * (x * x * x))))


def _silu(x):
    return x * _sigmoid(x)


def _neg_abs(x):
    bits = lax.bitcast_convert_type(x, jnp.uint32) | jnp.uint32(0x80000000)
    return lax.bitcast_convert_type(bits, jnp.float32)


def _dot(a, b):
    return jnp.dot(a, b, preferred_element_type=jnp.float32)


def _dot_nt(a, b):
    return lax.dot_general(a, b, _NT, preferred_element_type=jnp.float32)


def _dot_tn(a, b):
    return lax.dot_general(a, b, _TN, preferred_element_type=jnp.float32)


def _resident(shape):
    nd = len(shape)
    return pl.BlockSpec(shape, lambda *_: (0,) * nd, pipeline_mode=pl.Buffered(1))


def _resident_layer(stacked, l):
    nd = stacked.ndim
    return pl.BlockSpec((None,) + stacked.shape[1:], lambda *_: (l,) + (0,) * (nd - 1),
                        pipeline_mode=pl.Buffered(1))


def _level_halves(t):
    h = t // 2
    out = []
    while h >= 2:
        out.append(h)
        h //= 2
    return out


def _cumsum_matrix(t):
    tri = np.tril(np.ones((t, t), np.float32))
    return np.concatenate([tri, tri], axis=1)


def _level_masks(t):
    idx = np.arange(t)
    masks = []
    for h in _level_halves(t):
        blk = idx // (2 * h)
        second = (idx % (2 * h)) >= h
        masks.append((blk[:, None] == blk[None, :]) & second[:, None] & ~second[None, :])
    return np.stack(masks).astype(np.float32)


def _project_inputs(x_ref, g_ref, w_ref, lng_ref, lnb_ref, lb_ref, lguard_ref, omlb_ref,
                    hb_ref, zf_ref, zh_ref, kp_ref, lf_ref, *, d):
    blocks = []
    for c in BRANCH_ORDER:
        width = d if c == V else IN_PROJ_COLS
        blocks += [(c, c0, width) for c0 in range(0, d, width)]

    x = x_ref[...]
    h = x * lax.rsqrt(jnp.mean(x * x, axis=-1, keepdims=True) + EPS) * g_ref[...]
    hb_ref[...] = _bf16(h)

    def matmul(c, c0, width):
        return _dot(hb_ref[...], w_ref[:, c * d + c0:c * d + c0 + width])

    def epilogue(c, c0, width, z):
        cs = slice(c0, c0 + width)
        if c == U:
            out = _gelu(z)
        elif c == V:
            gv = _gelu(z)
            mu = jnp.mean(gv, axis=-1, keepdims=True)
            xc = gv - mu
            var = jnp.mean(xc * xc, axis=-1, keepdims=True)
            out = xc * lax.rsqrt(var + EPS) * lng_ref[...] + lnb_ref[...]
        elif c in (Q, G):
            out = _silu(z)
        elif c == K:
            ez = jnp.exp(_neg_abs(z))
            r = 1.0 / (1.0 + ez)
            er = ez * r
            pos = z >= 0.0
            f = lb_ref[:, cs] + omlb_ref[:, cs] * jnp.where(pos, r, er)
            lf_ref[:, cs] = jnp.maximum(jnp.log(f), lguard_ref[:, cs] + jnp.minimum(z, 0.0))
            out = omlb_ref[:, cs] * jnp.where(pos, er, r)
        elif c == I:
            out = z
        else:
            out = _sigmoid(z)
        if c in BF16_SLOT:
            zh_ref[BF16_SLOT[c], :, cs] = _bf16(out)
        elif c == K:
            kp_ref[SUBLANES:, cs] = out
        else:
            zf_ref[F32_SLOT[c], :, cs] = out

    z = matmul(*blocks[0])
    for i, blk in enumerate(blocks):
        z_next = matmul(*blocks[i + 1]) if i + 1 < len(blocks) else None
        epilogue(*blk, z)
        z = z_next


def _mix_tokens(x_ref, tri_ref, mask_ref, ws_ref, bst_ref, ng_ref, wa_ref, wb_ref, wo_ref,
                out_ref, zf_ref, zh_ref, kp_ref, lf_ref, state_ref, wsm_ref, bc_ref, o_ref,
                pa_ref, a_ref, b_ref):
    rows = x_ref.shape[0]
    t = CHUNK
    halves = _level_halves(t)

    @pl.when((pl.program_id(0) == 0) & (pl.program_id(1) == 0))
    def _():
        tok = lax.broadcasted_iota(jnp.int32, (t, t), 0)
        src = lax.broadcasted_iota(jnp.int32, (t, t), 1)
        for g in range(GROUPS):
            wsm_ref[g] = _bf16(jnp.where(tok >= src, ws_ref[g], 0.0))

    @pl.when(pl.program_id(1) == 0)
    def _():
        state_ref[...] = jnp.zeros_like(state_ref)

    for r0 in range(0, rows, t):
        lf = lf_ref[r0:r0 + t, :]
        hi = _bf16(lf)
        lo = _bf16(lf - _f32(hi))
        bc_ref[r0:r0 + t, :] = LOG2E * _dot(tri_ref[...], jnp.concatenate([hi, lo], axis=0))

    sub = lax.broadcasted_iota(jnp.int32, (t, HEAD), 0)
    tok = lax.broadcasted_iota(jnp.int32, (t, t), 0)
    src = lax.broadcasted_iota(jnp.int32, (t, t), 1)
    eye = tok == src
    pair = (tok - 1 == src) & ((tok & 1) == 1)

    def bcast_row(r, n, hs):
        return jnp.broadcast_to(bc_ref[r:r + 1, hs], (n, HEAD))

    def act(c, r0, g):
        hs = slice(g * HEAD, (g + 1) * HEAD)
        if c in BF16_SLOT:
            return zh_ref[BF16_SLOT[c], r0:r0 + t, hs]
        if c == K:
            return kp_ref[SUBLANES + r0:SUBLANES + r0 + t, hs]
        return zf_ref[F32_SLOT[c], r0:r0 + t, hs]

    chunk_starts = list(range(0, rows, t))
    for g in range(GROUPS):
        mixed = _dot(wsm_ref[g], jnp.concatenate([act(V, r0, g) for r0 in chunk_starts], axis=1))
        for ci, r0 in enumerate(chunk_starts):
            m_c = mixed[:, ci * HEAD:(ci + 1) * HEAD] + bst_ref[:, g:g + 1]
            a_ref[r0:r0 + t, g * HEAD:(g + 1) * HEAD] = _bf16(act(U, r0, g) * m_c)

    pa_ref[...] = _dot(a_ref[...], wa_ref[...])

    units = [(r0, g) for r0 in range(0, rows, t) for g in range(GROUPS)]
    srows = {u: [None] * (t // SUBLANES) for u in units}

    def add_rows(u, row0, s):
        for j in range(s.shape[0] // SUBLANES):
            i = row0 // SUBLANES + j
            piece = s[j * SUBLANES:(j + 1) * SUBLANES]
            srows[u][i] = piece if srows[u][i] is None else srows[u][i] + piece

    def level_scores(u, lev, h):
        r0, g = u
        hs = slice(g * HEAD, (g + 1) * HEAD)
        bc = bc_ref[r0:r0 + t, hs]
        q = act(Q, r0, g)
        k = act(K, r0, g)
        n_blk = t // (2 * h)
        if h >= SUBLANES:
            q_parts, k_parts = [], []
            for blk in range(n_blk):
                b0 = blk * 2 * h
                ref = bcast_row(r0 + b0 + h - 1, h, hs)
                q_parts.append(q[b0 + h:b0 + 2 * h] * jnp.exp2(bc[b0 + h:b0 + 2 * h] - ref))
                k_parts.append(k[b0:b0 + h] * jnp.exp2(ref - bc[b0:b0 + h]))
                k_parts.append(jnp.zeros((h, HEAD), jnp.float32))
            s = _dot_nt(_bf16(jnp.concatenate(q_parts, axis=0)),
                        _bf16(jnp.concatenate(k_parts, axis=0)))
            if n_blk > 1:
                s = s * jnp.concatenate(
                    [mask_ref[lev, blk * 2 * h + h:(blk + 1) * 2 * h, :]
                     for blk in range(n_blk)], axis=0)
            for blk in range(n_blk):
                add_rows(u, blk * 2 * h + h, s[blk * h:(blk + 1) * h])
        else:
            parts = []
            for v0 in range(0, t, SUBLANES):
                if 2 * h == SUBLANES:
                    parts.append(bcast_row(r0 + v0 + h - 1, SUBLANES, hs))
                else:
                    first = bcast_row(r0 + v0 + h - 1, SUBLANES, hs)
                    second = bcast_row(r0 + v0 + 2 * h + h - 1, SUBLANES, hs)
                    parts.append(jnp.where((sub[:SUBLANES] & (2 * h)) == 0, first, second))
            e = jnp.exp2(_neg_abs(bc - jnp.concatenate(parts, axis=0)))
            add_rows(u, 0, _dot_nt(_bf16(q * e), _bf16(k * e)) * mask_ref[lev])

    for lev, h in enumerate(halves):
        for u in units:
            level_scores(u, lev, h)

    for u in units:
        r0, g = u
        hs = slice(g * HEAD, (g + 1) * HEAD)
        q = act(Q, r0, g)
        k = act(K, r0, g)
        vb = act(I, r0, g)
        bc = bc_ref[r0:r0 + t, hs]
        diag = jnp.sum(q * k, axis=-1, keepdims=True)
        k_prev = kp_ref[SUBLANES - 1 + r0:SUBLANES - 1 + r0 + t, hs]
        near = jnp.sum(q * (1.0 - k) * k_prev, axis=-1, keepdims=True)
        scores = jnp.where(eye, diag, jnp.where(pair, near, jnp.concatenate(srows[u], axis=0)))
        st = state_ref[g]
        q_in = _bf16(q * jnp.exp2(bc))
        o_ref[r0:r0 + t, hs] = _dot(jnp.concatenate([_bf16(scores), q_in], axis=1),
                                    jnp.concatenate([vb, _bf16(st)], axis=0))
        b_last = bcast_row(r0 + t - 1, t, hs)
        k_end = _bf16(k * jnp.exp2(b_last - bc))
        decay = jnp.exp2(bc_ref[r0 + t - 1:r0 + t, hs])
        state_ref[g] = (st * jnp.transpose(jnp.broadcast_to(decay, (HEAD, HEAD)))
                        + _dot_tn(k_end, vb))

    for g in range(GROUPS):
        hs = slice(g * HEAD, (g + 1) * HEAD)
        o = o_ref[:, hs]
        o = o * lax.rsqrt(jnp.mean(o * o, axis=-1, keepdims=True) + EPS) * ng_ref[...]
        b_ref[:, hs] = _bf16(o * zf_ref[F32_SLOT[G], :, hs])

    pb = _dot(b_ref[...], wb_ref[...])
    merged = zf_ref[F32_SLOT[GATE_A]] * pa_ref[...] + zf_ref[F32_SLOT[GATE_B]] * pb
    out_ref[...] = x_ref[...] + _dot(_bf16(merged), wo_ref[...])


def _mixer_kernel(x_ref, g_ref, w_ref, lng_ref, lnb_ref, lb_ref, lguard_ref, omlb_ref,
                  tri_ref, mask_ref, ws_ref, bst_ref, ng_ref, wa_ref, wb_ref, wo_ref,
                  out_ref,
                  hb_ref, zf_ref, zh_ref, kp_ref, lf_ref, state_ref, wsm_ref, bc_ref, o_ref,
                  pa_ref, a_ref, b_ref, *, d):
    @pl.when((pl.program_id(0) == 0) & (pl.program_id(1) == 0))
    def _():
        kp_ref[0:SUBLANES, :] = jnp.zeros((SUBLANES, d), jnp.float32)

    _project_inputs(x_ref, g_ref, w_ref, lng_ref, lnb_ref, lb_ref, lguard_ref, omlb_ref,
                    hb_ref, zf_ref, zh_ref, kp_ref, lf_ref, d=d)
    _mix_tokens(x_ref, tri_ref, mask_ref, ws_ref, bst_ref, ng_ref, wa_ref, wb_ref, wo_ref,
                out_ref, zf_ref, zh_ref, kp_ref, lf_ref, state_ref, wsm_ref, bc_ref, o_ref,
                pa_ref, a_ref, b_ref)


def _mixer(x2, g, w, lng, lnb, lb, lguard, omlb, tri, masks, ws, bst, ng, wa, wb, wo, *,
           batch, layer):
    m, d = x2.shape
    rows = MIXER_ROWS
    steps = m // batch // rows
    row = lambda b, s: (b * steps + s, 0)
    vec = _resident((1, d))
    return pl.pallas_call(
        functools.partial(_mixer_kernel, d=d),
        grid=(batch, steps),
        in_specs=[
            pl.BlockSpec((rows, d), row),
            vec, _resident_layer(w, layer), vec, vec, vec, vec, vec,
            _resident(tri.shape), _resident(masks.shape), _resident_layer(ws, layer),
            _resident(bst.shape), _resident(ng.shape),
            _resident_layer(wa, layer), _resident_layer(wb, layer), _resident_layer(wo, layer),
        ],
        out_specs=pl.BlockSpec((rows, d), row),
        out_shape=jax.ShapeDtypeStruct((m, d), jnp.float32),
        scratch_shapes=[
            pltpu.VMEM((rows, d), jnp.bfloat16),
            pltpu.VMEM((len(F32_SLOT), rows, d), jnp.float32),
            pltpu.VMEM((len(BF16_SLOT), rows, d), jnp.bfloat16),
            pltpu.VMEM((SUBLANES + rows, d), jnp.float32),
            pltpu.VMEM((rows, d), jnp.float32),
            pltpu.VMEM((GROUPS, HEAD, HEAD), jnp.float32),
            pltpu.VMEM((GROUPS, CHUNK, CHUNK), jnp.bfloat16),
            pltpu.VMEM((rows, d), jnp.float32),
            pltpu.VMEM((rows, d), jnp.float32),
            pltpu.VMEM((rows, d), jnp.float32),
            pltpu.VMEM((rows, d), jnp.bfloat16),
            pltpu.VMEM((rows, d), jnp.bfloat16),
        ],
        compiler_params=pltpu.CompilerParams(
            dimension_semantics=("arbitrary", "arbitrary"), vmem_limit_bytes=VMEM_LIMIT),
        name="mixer",
    )(x2, g, w, lng, lnb, lb, lguard, omlb, tri, masks, ws, bst, ng, wa, wb, wo)


def _mlp_kernel(x_ref, g_ref, wu_ref, wd_ref, fg_ref, out_ref, *, final_norm):
    rows = x_ref.shape[0]
    for r0 in range(0, rows, ROW_BLOCK):
        rs = slice(r0, r0 + ROW_BLOCK)
        x = x_ref[rs, :]
        h = x * lax.rsqrt(jnp.mean(x * x, axis=-1, keepdims=True) + EPS) * g_ref[...]
        up = jnp.maximum(_dot(_bf16(h), wu_ref[...]), 0.0)
        y = x + _dot(_bf16(up * up), wd_ref[...])
        if final_norm:
            y = y * lax.rsqrt(jnp.mean(y * y, axis=-1, keepdims=True) + EPS) * fg_ref[...]
        out_ref[rs, :] = y


def _mlp(x2, g, wu, wd, fg, *, final_norm, layer):
    m, d = x2.shape
    rows = min(MLP_ROWS, m)
    row = lambda i: (i, 0)
    return pl.pallas_call(
        functools.partial(_mlp_kernel, final_norm=final_norm),
        grid=(m // rows,),
        in_specs=[pl.BlockSpec((rows, d), row), _resident((1, d)), _resident_layer(wu, layer),
                  _resident_layer(wd, layer), _resident((1, d))],
        out_specs=pl.BlockSpec((rows, d), row),
        out_shape=jax.ShapeDtypeStruct((m, d), jnp.float32),
        compiler_params=pltpu.CompilerParams(
            dimension_semantics=("arbitrary",), vmem_limit_bytes=VMEM_LIMIT),
        name="mlp",
    )(x2, g, wu, wd, fg)


def kernel(x, mix_norm_g, w_in, sgu_norm_g, sgu_norm_b, w_spatial, b_spatial, lower_bounds,
           hgrn_norm_g, w_branch_a, w_branch_b, w_out, mlp_norm_g, w_mlp_up, w_mlp_down,
           final_norm_g):
    batch, seq, d = x.shape
    depth = w_in.shape[0]
    assert d == GROUPS * HEAD and seq % MIXER_ROWS == 0 and w_in.shape[2] == N_BRANCH * d

    lb = jnp.cumsum(jax.nn.softmax(_f32(lower_bounds), axis=0), axis=0)
    lb = lb - lb[0:1]
    lguard = jnp.log1p(-lb) - math.log(2.0)

    tri = jnp.asarray(_cumsum_matrix(CHUNK), jnp.bfloat16)
    masks = jnp.asarray(_level_masks(CHUNK), jnp.float32)

    w_in_b, wa_b, wb_b, wo_b = (_bf16(w) for w in (w_in, w_branch_a, w_branch_b, w_out))
    wu_b, wd_b = _bf16(w_mlp_up), _bf16(w_mlp_down)

    row = lambda p, l: p[l].reshape(1, -1)
    x2 = x.reshape(batch * seq, d)
    for l in range(depth):
        x2 = _mixer(x2, row(mix_norm_g, l), w_in_b, row(sgu_norm_g, l), row(sgu_norm_b, l),
                    row(lb, l), row(lguard, l), row(1.0 - lb, l), tri, masks, w_spatial,
                    b_spatial[l].T, row(hgrn_norm_g, l), wa_b, wb_b, wo_b,
                    batch=batch, layer=l)
        x2 = _mlp(x2, row(mlp_norm_g, l), wu_b, wd_b, final_norm_g.reshape(1, -1),
                  final_norm=(l == depth - 1), layer=l)
    return x2.reshape(batch, seq, d)
```
